```python
import jax
import jax.numpy as jnp
from jax import lax
import numpy as np

D_MODEL = 2048
BATCH = 4
SEQ = 2048
DEPTH = 4

N_MIXERS = 3
N_POOL_LAYERS = (DEPTH + 2) // 3
N_SGU_LAYERS = (DEPTH + 1) // 3
N_NSA_LAYERS = DEPTH // 3

RMS_EPS = 1e-6
LN_EPS = 1e-5
NEG = -1e30
BIG = 1e30

FFN_DIM = ((8 * D_MODEL // 3 + 255) // 256) * 256
CONV_WIDTH = 3

POOL_DIM = D_MODEL
POOL_WINDOWS = (2, 4, 8, 16)
POOL_GROUPS = len(POOL_WINDOWS)
POOL_GROUP_DIM = POOL_DIM // POOL_GROUPS

SGU_DIM = D_MODEL
SGU_CHUNK = 128
SGU_GROUPS = 16
SGU_GROUP_DIM = SGU_DIM // SGU_GROUPS

HEAD_DIM = 128
N_HEADS = D_MODEL // HEAD_DIM
N_KV_HEADS = 4
GQA_REP = N_HEADS // N_KV_HEADS
ROPE_THETA = 10000.0
CMP_BLOCK = 32
CMP_STRIDE = 16
CMP_HIDDEN = 2 * HEAD_DIM
SLC_BLOCK = 64
SLC_TOPK = 16
SLC_QBLOCK = 16
WIN = 512
WIN_QBLOCK = 128
NSA_Q_DIM = N_HEADS * HEAD_DIM
NSA_KV_DIM = N_KV_HEADS * HEAD_DIM
NSA_N_GATES = 3
NSA_IN_DIM = NSA_Q_DIM + 6 * NSA_KV_DIM + NSA_N_GATES * N_HEADS

kernel_name = 'hybrid_pool_sgu_nsa_trunk'


def rmsnorm(x, g):
    xf = x.astype(jnp.float32)
    y = xf * lax.rsqrt(jnp.mean(xf * xf, axis=-1, keepdims=True) + RMS_EPS)
    return (y * g.astype(jnp.float32)).astype(x.dtype)


def layernorm(x, g, b):
    xf = x.astype(jnp.float32)
    mu = jnp.mean(xf, axis=-1, keepdims=True)
    var = jnp.mean(jnp.square(xf - mu), axis=-1, keepdims=True)
    y = (xf - mu) * lax.rsqrt(var + LN_EPS)
    return (y * g.astype(jnp.float32) + b.astype(jnp.float32)).astype(x.dtype)


def rope(x, pos):
    half = HEAD_DIM // 2
    inv = 1.0 / (ROPE_THETA ** (jnp.arange(half, dtype=jnp.float32) / half))
    ang = pos.astype(jnp.float32)[:, None] * inv[None, :]
    cos = jnp.cos(ang)[None, :, None, :]
    sin = jnp.sin(ang)[None, :, None, :]
    x1 = x[..., :half].astype(jnp.float32)
    x2 = x[..., half:].astype(jnp.float32)
    return jnp.concatenate([x1 * cos - x2 * sin, x2 * cos + x1 * sin], axis=-1).astype(x.dtype)


def pool_mixer(h, w_in, w_grp, scale, w_out):
    B, T, _ = h.shape
    z = (h @ w_in).astype(jnp.float32)
    cs = jnp.pad(jnp.cumsum(z, axis=1), ((0, 0), (1, 0), (0, 0)))
    t = jnp.arange(T)
    outs = []
    for g, w in enumerate(POOL_WINDOWS):
        sl = slice(g * POOL_GROUP_DIM, (g + 1) * POOL_GROUP_DIM)
        start = jnp.maximum(t + 1 - w, 0)
        win_sum = cs[:, 1:, sl] - cs[:, start, sl]
        cnt = (t + 1 - start).astype(jnp.float32)[None, :, None]
        outs.append(win_sum / cnt - z[:, :, sl])
    p = jnp.stack(outs, axis=2).astype(h.dtype)
    m = jnp.einsum('btgc,gcd->btgd', p, w_grp).reshape(B, T, POOL_DIM)
    return (m * scale) @ w_out


def sgu_mixer(h, w_in, ln_g, ln_b, w_s, b_s, w_out):
    B, T, _ = h.shape
    u, v = jnp.split(jax.nn.gelu(h @ w_in), 2, axis=-1)
    v = layernorm(v, ln_g, ln_b)
    n_chunks = T // SGU_CHUNK
    v = v.reshape(B, n_chunks, SGU_CHUNK, SGU_GROUPS, SGU_GROUP_DIM)
    causal = jnp.tril(jnp.ones((SGU_CHUNK, SGU_CHUNK), dtype=bool))
    ws = jnp.where(causal[None], w_s, jnp.zeros((), w_s.dtype))
    mixed = jnp.einsum('gts,bcsgd->bctgd', ws, v) + b_s.T[None, None, :, :, None]
    return (u * mixed.reshape(B, T, SGU_DIM)) @ w_out


def compress(a, pe, w1, b1, w2, b2):
    B, T = a.shape[:2]
    nb = T // CMP_STRIDE
    r = CMP_BLOCK // CMP_STRIDE
    nc = nb - r + 1
    ab = a.reshape(B, nb, CMP_STRIDE, N_KV_HEADS, HEAD_DIM)
    win = jnp.concatenate([ab[:, j:j + nc] for j in range(r)], axis=2)
    win = win + pe[None, None, :, None, :]
    flat = jnp.swapaxes(win, 2, 3).reshape(B, nc, N_KV_HEADS, CMP_BLOCK * HEAD_DIM)
    return jax.nn.gelu(flat @ w1 + b1) @ w2 + b2


def nsa_mixer(h, w_in, gate_b, cmp_pe, cmp_w1, cmp_b1, cmp_w2, cmp_b2, w_out):
    B, T, _ = h.shape
    pos = jnp.arange(T)
    scale = HEAD_DIM ** -0.5
    splits = np.cumsum([NSA_Q_DIM] + [NSA_KV_DIM] * 6).tolist()
    q, kc, vc, ks, vs, kw, vw, g = jnp.split(h @ w_in, splits, axis=-1)
    kv_shape = (B, T, N_KV_HEADS, HEAD_DIM)
    q = rope(q.reshape(B, T, N_HEADS, HEAD_DIM), pos)
    kc = rope(kc.reshape(kv_shape), pos)
    ks = rope(ks.reshape(kv_shape), pos)
    kw = rope(kw.reshape(kv_shape), pos)
    vc, vs, vw = vc.reshape(kv_shape), vs.reshape(kv_shape), vw.reshape(kv_shape)
    gates = jax.nn.sigmoid((g + gate_b).astype(jnp.float32)).astype(h.dtype)
    gates = gates.reshape(B, T, N_KV_HEADS, GQA_REP, NSA_N_GATES)
    qg = q.reshape(B, T, N_KV_HEADS, GQA_REP, HEAD_DIM)

    kcmp = compress(kc, cmp_pe[0], cmp_w1[0], cmp_b1[0], cmp_w2[0], cmp_b2[0])
    vcmp = compress(vc, cmp_pe[1], cmp_w1[1], cmp_b1[1], cmp_w2[1], cmp_b2[1])
    nc = kcmp.shape[1]
    s_c = jnp.einsum('btgrd,bcgd->bgrtc', qg, kcmp).astype(jnp.float32) * scale
    c_end = jnp.arange(nc) * CMP_STRIDE + CMP_BLOCK - 1
    m_c = c_end[None, :] <= pos[:, None]
    s_c = jnp.where(m_c, s_c, NEG)
    p_c = jax.nn.softmax(s_c, axis=-1) * jnp.any(m_c, axis=-1)[:, None].astype(jnp.float32)
    o_c = jnp.einsum('bgrtc,bcgd->btgrd', p_c.astype(vcmp.dtype), vcmp)

    ns = T // SLC_BLOCK
    ci = np.arange(nc)[:, None]
    sj = np.arange(ns)[None, :]
    overlap = (ci * CMP_STRIDE <= (sj + 1) * SLC_BLOCK - 1) & (ci * CMP_STRIDE + CMP_BLOCK - 1 >= sj * SLC_BLOCK)
    imp = jnp.einsum('bgrtc,cs->bgts', p_c, jnp.asarray(overlap.astype(np.float32)))
    blk = jnp.arange(ns)[None, :]
    cur = (pos // SLC_BLOCK)[:, None]
    forced = (blk == 0) | (blk == cur) | (blk == cur - 1)
    imp = jnp.where(forced, BIG, imp)
    imp = jnp.where(blk <= cur, imp, NEG)
    n_sel = min(SLC_TOPK, ns)
    top_s, top_i = lax.top_k(imp, n_sel)
    sel_ok = top_s > 0.5 * NEG

    kblk = ks.reshape(B, ns, SLC_BLOCK, N_KV_HEADS, HEAD_DIM).transpose(0, 3, 1, 2, 4)
    vblk = vs.reshape(B, ns, SLC_BLOCK, N_KV_HEADS, HEAD_DIM).transpose(0, 3, 1, 2, 4)
    nq = T // SLC_QBLOCK
    qm = qg.reshape(B, nq, SLC_QBLOCK, N_KV_HEADS, GQA_REP, HEAD_DIM).transpose(1, 0, 3, 2, 4, 5)
    im = top_i.reshape(B, N_KV_HEADS, nq, SLC_QBLOCK, n_sel).transpose(2, 0, 1, 3, 4)
    okm = sel_ok.reshape(B, N_KV_HEADS, nq, SLC_QBLOCK, n_sel).transpose(2, 0, 1, 3, 4)
    pm = pos.reshape(nq, SLC_QBLOCK)
    gather = jax.vmap(jax.vmap(lambda blocks, ix: blocks[ix]))
    offs = jnp.arange(SLC_BLOCK)

    def slc_block(args):
        qb, ib, okb, pb = args
        kg = gather(kblk, ib)
        vg = gather(vblk, ib)
        s = jnp.einsum('bgtrd,bgtnld->bgtrnl', qb, kg).astype(jnp.float32) * scale
        kpos = ib[..., None] * SLC_BLOCK + offs
        m = okb[..., None] & (kpos <= pb[None, None, :, None, None])
        s = jnp.where(m[:, :, :, None], s, NEG)
        p = jax.nn.softmax(s.reshape(s.shape[:4] + (-1,)), axis=-1).reshape(s.shape)
        return jnp.einsum('bgtrnl,bgtnld->bgtrd', p.astype(vg.dtype), vg)

    o_s = lax.map(slc_block, (qm, im, okm, pm))
    o_s = o_s.transpose(1, 0, 3, 2, 4, 5).reshape(B, T, N_KV_HEADS, GQA_REP, HEAD_DIM)

    nb = T // WIN_QBLOCK
    nband = WIN // WIN_QBLOCK + 1

    def band(a):
        ap = jnp.pad(a, ((0, 0), (WIN, 0), (0, 0), (0, 0))).reshape(B, nb + nband - 1, WIN_QBLOCK, N_KV_HEADS, HEAD_DIM)
        return jnp.concatenate([ap[:, j:j + nb] for j in range(nband)], axis=2)

    kband, vband = band(kw), band(vw)
    qw = qg.reshape(B, nb, WIN_QBLOCK, N_KV_HEADS, GQA_REP, HEAD_DIM)
    s_w = jnp.einsum('bntgrd,bnkgd->bngrtk', qw, kband).astype(jnp.float32) * scale
    qpos = jnp.arange(nb)[:, None] * WIN_QBLOCK + jnp.arange(WIN_QBLOCK)[None, :]
    kpos = jnp.arange(nb)[:, None] * WIN_QBLOCK - WIN + jnp.arange(nband * WIN_QBLOCK)[None, :]
    diff = qpos[:, :, None] - kpos[:, None, :]
    m_w = (diff >= 0) & (diff < WIN) & (kpos[:, None, :] >= 0)
    s_w = jnp.where(m_w[None, :, None, None], s_w, NEG)
    p_w = jax.nn.softmax(s_w, axis=-1)
    o_w = jnp.einsum('bngrtk,bnkgd->bntgrd', p_w.astype(vband.dtype), vband)
    o_w = o_w.reshape(B, T, N_KV_HEADS, GQA_REP, HEAD_DIM)

    o = gates[..., 0:1] * o_c + gates[..., 1:2] * o_s + gates[..., 2:3] * o_w
    return o.reshape(B, T, NSA_Q_DIM) @ w_out


def conv_ffn(h, w_up, conv_w, conv_b, w_down):
    a, b = jnp.split(h @ w_up, 2, axis=-1)
    a = lax.conv_general_dilated(a, conv_w[:, None, :], window_strides=(1,),
                                 padding=[(CONV_WIDTH - 1, 0)],
                                 dimension_numbers=('NWC', 'WIO', 'NWC'),
                                 feature_group_count=FFN_DIM) + conv_b
    return (jax.nn.silu(a) * b) @ w_down


def setup_inputs(seed: int = 0) -> dict:
    key = jax.random.key(seed)
    keys = jax.random.split(key, 32)
    counter = [0]

    def nrm(shape, s):
        k = keys[counter[0]]
        counter[0] += 1
        return s * jax.random.normal(k, shape, jnp.float32)

    D = D_MODEL
    return {
        'x': nrm((BATCH, SEQ, D), 1.0),
        'norm_g': 1.0 + nrm((DEPTH, 4, D), 0.02),
        'ffn_w_up': nrm((DEPTH, D, 2 * FFN_DIM), D ** -0.5),
        'ffn_conv_w': nrm((DEPTH, CONV_WIDTH, FFN_DIM), CONV_WIDTH ** -0.5),
        'ffn_conv_b': nrm((DEPTH, FFN_DIM), 0.01),
        'ffn_w_down': nrm((DEPTH, FFN_DIM, D), FFN_DIM ** -0.5),
        'pool_w_in': nrm((N_POOL_LAYERS, D, POOL_DIM), D ** -0.5),
        'pool_w_grp': nrm((N_POOL_LAYERS, POOL_GROUPS, POOL_GROUP_DIM, POOL_GROUP_DIM), POOL_GROUP_DIM ** -0.5),
        'pool_scale': 1.0 + nrm((N_POOL_LAYERS, POOL_DIM), 0.02),
        'pool_w_out': nrm((N_POOL_LAYERS, POOL_DIM, D), POOL_DIM ** -0.5),
        'sgu_w_in': nrm((N_SGU_LAYERS, D, 2 * SGU_DIM), D ** -0.5),
        'sgu_ln_g': 1.0 + nrm((N_SGU_LAYERS, SGU_DIM), 0.02),
        'sgu_ln_b': nrm((N_SGU_LAYERS, SGU_DIM), 0.01),
        'sgu_w_s': nrm((N_SGU_LAYERS, SGU_GROUPS, SGU_CHUNK, SGU_CHUNK), SGU_CHUNK ** -0.5),
        'sgu_b_s': 1.0 + nrm((N_SGU_LAYERS, SGU_GROUPS, SGU_CHUNK), 0.02),
        'sgu_w_out': nrm((N_SGU_LAYERS, SGU_DIM, D), SGU_DIM ** -0.5),
        'nsa_w_in': nrm((N_NSA_LAYERS, D, NSA_IN_DIM), D ** -0.5),
        'nsa_gate_b': nrm((N_NSA_LAYERS, NSA_N_GATES * N_HEADS), 0.01),
        'nsa_cmp_pe': nrm((N_NSA_LAYERS, 2, CMP_BLOCK, HEAD_DIM), 0.02),
        'nsa_cmp_w1': nrm((N_NSA_LAYERS, 2, CMP_BLOCK * HEAD_DIM, CMP_HIDDEN), (CMP_BLOCK * HEAD_DIM) ** -0.5),
        'nsa_cmp_b1': nrm((N_NSA_LAYERS, 2, CMP_HIDDEN), 0.01),
        'nsa_cmp_w2': nrm((N_NSA_LAYERS, 2, CMP_HIDDEN, HEAD_DIM), CMP_HIDDEN ** -0.5),
        'nsa_cmp_b2': nrm((N_NSA_LAYERS, 2, HEAD_DIM), 0.01),
        'nsa_w_out': nrm((N_NSA_LAYERS, NSA_Q_DIM, D), NSA_Q_DIM ** -0.5),
    }


def reference(x, norm_g, ffn_w_up, ffn_conv_w, ffn_conv_b, ffn_w_down,
              pool_w_in, pool_w_grp, pool_scale, pool_w_out,
              sgu_w_in, sgu_ln_g, sgu_ln_b, sgu_w_s, sgu_b_s, sgu_w_out,
              nsa_w_in, nsa_gate_b, nsa_cmp_pe, nsa_cmp_w1, nsa_cmp_b1, nsa_cmp_w2, nsa_cmp_b2, nsa_w_out):
    h = x
    for i in range(DEPTH):
        kind, j = i % N_MIXERS, i // N_MIXERS
        u = rmsnorm(h, norm_g[i, 0])
        if kind == 0:
            m = pool_mixer(u, pool_w_in[j], pool_w_grp[j], pool_scale[j], pool_w_out[j])
        elif kind == 1:
            m = sgu_mixer(u, sgu_w_in[j], sgu_ln_g[j], sgu_ln_b[j], sgu_w_s[j], sgu_b_s[j], sgu_w_out[j])
        else:
            m = nsa_mixer(u, nsa_w_in[j], nsa_gate_b[j], nsa_cmp_pe[j], nsa_cmp_w1[j], nsa_cmp_b1[j],
                          nsa_cmp_w2[j], nsa_cmp_b2[j], nsa_w_out[j])
        h = h + rmsnorm(m, norm_g[i, 1])
        f = conv_ffn(rmsnorm(h, norm_g[i, 2]), ffn_w_up[i], ffn_conv_w[i], ffn_conv_b[i], ffn_w_down[i])
        h = h + rmsnorm(f, norm_g[i, 3])
    return h
```

```python
import functools
import math

import jax
import jax.numpy as jnp
import numpy as np
from jax import lax
from jax.experimental import pallas as pl
from jax.experimental.pallas import tpu as pltpu

F32 = jnp.float32
BF16 = jnp.bfloat16

D_MODEL = 2048
DEPTH = 4
RMS_EPS = 1e-6
LN_EPS = 1e-5
NEG = -1e30
BIG = 1e30

FFN_DIM = 5632
POOL_WINDOWS = (2, 4, 8, 16)
POOL_GROUP_DIM = D_MODEL // len(POOL_WINDOWS)
SGU_CHUNK = 128
SGU_GROUPS = 16
SGU_GROUP_DIM = D_MODEL // SGU_GROUPS

HEAD_DIM = 128
N_HEADS = 16
N_KV_HEADS = 4
GQA_REP = N_HEADS // N_KV_HEADS
ROPE_THETA = 10000.0
CMP_BLOCK = 32
CMP_STRIDE = 16
CMP_HIDDEN = 2 * HEAD_DIM
SLC_BLOCK = 64
SLC_TOPK = 16
WIN = 512
NSA_Q_DIM = N_HEADS * HEAD_DIM
NSA_KV_DIM = N_KV_HEADS * HEAD_DIM
N_GATES = 3

LANES = 128
BF16_SUBLANES = 16
VMEM_LIMIT = 56 * 1024 * 1024

ROW_TILE = 512
MIX_ROW_TILE = 256
FFN_COL_TILE = 512
ATT_TILE = 128


def _cparams(*sem):
    return pltpu.CompilerParams(dimension_semantics=sem, vmem_limit_bytes=VMEM_LIMIT)


def _const_spec(shape):
    nd = len(shape)
    return pl.BlockSpec(shape, lambda *_: (0,) * nd, pipeline_mode=pl.Buffered(1))


def _rms(x):
    return x * lax.rsqrt(jnp.mean(x * x, axis=-1, keepdims=True) + RMS_EPS)


def _residual_update(m, h_ref, ga_ref, gb_ref, ho_ref, uo_ref):
    hn = h_ref[...] + _rms(m) * ga_ref[...]
    ho_ref[...] = hn
    uo_ref[...] = (_rms(hn) * gb_ref[...]).astype(BF16)


def _dot(a, b):
    return jnp.dot(a, b, preferred_element_type=F32)


def _dot_nt(a, b):
    return lax.dot_general(a, b, (((1,), (1,)), ((), ())), preferred_element_type=F32)


def _halo_index(tm):
    blocks = tm // BF16_SUBLANES
    return lambda i, *_: (jnp.maximum(i * blocks - 1, 0), 0)


def _norm_kernel(h_ref, g_ref, u_ref):
    u_ref[...] = (_rms(h_ref[...]) * g_ref[...]).astype(BF16)


def _first_norm(h, g):
    n = h.shape[0]
    return pl.pallas_call(
        _norm_kernel,
        grid=(n // ROW_TILE,),
        in_specs=[pl.BlockSpec((ROW_TILE, D_MODEL), lambda i: (i, 0)),
                  pl.BlockSpec((1, D_MODEL), lambda i: (0, 0))],
        out_specs=pl.BlockSpec((ROW_TILE, D_MODEL), lambda i: (i, 0)),
        out_shape=jax.ShapeDtypeStruct((n, D_MODEL), BF16),
        compiler_params=_cparams("parallel"),
        name="first_norm",
    )(h, g)


def _out_proj_kernel(x_ref, w_ref, h_ref, ga_ref, gb_ref, ho_ref, uo_ref):
    _residual_update(_dot(x_ref[...], w_ref[...]), h_ref, ga_ref, gb_ref, ho_ref, uo_ref)


def _out_proj(xin, w, h, ga, gb):
    n, k = xin.shape
    tm = ROW_TILE
    row = lambda i: (i, 0)
    return pl.pallas_call(
        _out_proj_kernel,
        grid=(n // tm,),
        in_specs=[pl.BlockSpec((tm, k), row), _const_spec((k, D_MODEL)),
                  pl.BlockSpec((tm, D_MODEL), row),
                  pl.BlockSpec((1, D_MODEL), lambda i: (0, 0)),
                  pl.BlockSpec((1, D_MODEL), lambda i: (0, 0))],
        out_specs=[pl.BlockSpec((tm, D_MODEL), row), pl.BlockSpec((tm, D_MODEL), row)],
        out_shape=[jax.ShapeDtypeStruct((n, D_MODEL), F32),
                   jax.ShapeDtypeStruct((n, D_MODEL), BF16)],
        compiler_params=_cparams("parallel"),
        name="out_proj",
    )(xin, w, h, ga, gb)


def _ffn_kernel(u_ref, uh_ref, wa_ref, wb_ref, cw_ref, cb_ref, wd_ref, h_ref, ga_ref, gb_ref,
                ho_ref, uo_ref, acc_ref, *, seq_tiles):
    i = pl.program_id(0)
    c = pl.program_id(1)

    @pl.when(c == 0)
    def _():
        acc_ref[...] = jnp.zeros_like(acc_ref)

    u = u_ref[...]
    wa = wa_ref[...]
    a = _dot(u, wa)
    b = _dot(u, wb_ref[...])
    ah = _dot(uh_ref[...], wa)
    ah = jnp.where(i % seq_tiles == 0, 0.0, ah)
    row = lax.broadcasted_iota(jnp.int32, a.shape, 0)
    last1 = ah[BF16_SUBLANES - 1:BF16_SUBLANES]
    last2 = ah[BF16_SUBLANES - 2:BF16_SUBLANES - 1]
    p1 = jnp.where(row == 0, last1, pltpu.roll(a, 1, 0))
    p2 = jnp.where(row == 0, last2, jnp.where(row == 1, last1, pltpu.roll(a, 2, 0)))
    cw = cw_ref[...]
    y = cw[0:1] * p2 + cw[1:2] * p1 + cw[2:3] * a + cb_ref[...]
    gated = y / (1.0 + jnp.exp(-y)) * b
    acc_ref[...] += _dot(gated.astype(BF16), wd_ref[...])

    @pl.when(c == pl.num_programs(1) - 1)
    def _():
        _residual_update(acc_ref[...], h_ref, ga_ref, gb_ref, ho_ref, uo_ref)


def _conv_ffn(u, h, w_up, conv_w, conv_b, w_down, ga, gb, seq):
    n = u.shape[0]
    tm, tf = ROW_TILE, FFN_COL_TILE
    n_chunks = FFN_DIM // tf
    row = lambda i, c: (i, 0)
    vec = lambda i, c: (0, 0)
    return pl.pallas_call(
        functools.partial(_ffn_kernel, seq_tiles=seq // tm),
        grid=(n // tm, n_chunks),
        in_specs=[pl.BlockSpec((tm, D_MODEL), row),
                  pl.BlockSpec((BF16_SUBLANES, D_MODEL), _halo_index(tm)),
                  pl.BlockSpec((D_MODEL, tf), lambda i, c: (0, c)),
                  pl.BlockSpec((D_MODEL, tf), lambda i, c: (0, c + n_chunks)),
                  pl.BlockSpec((3, tf), lambda i, c: (0, c)),
                  pl.BlockSpec((1, tf), lambda i, c: (0, c)),
                  pl.BlockSpec((tf, D_MODEL), lambda i, c: (c, 0)),
                  pl.BlockSpec((tm, D_MODEL), row),
                  pl.BlockSpec((1, D_MODEL), vec),
                  pl.BlockSpec((1, D_MODEL), vec)],
        out_specs=[pl.BlockSpec((tm, D_MODEL), row), pl.BlockSpec((tm, D_MODEL), row)],
        out_shape=[jax.ShapeDtypeStruct((n, D_MODEL), F32),
                   jax.ShapeDtypeStruct((n, D_MODEL), BF16)],
        scratch_shapes=[pltpu.VMEM((tm, D_MODEL), F32)],
        compiler_params=_cparams("parallel", "arbitrary"),
        name="conv_ffn",
    )(u, u, w_up, w_up, conv_w, conv_b, w_down, h, ga, gb)


def _pool_kernel(u_ref, uh_ref, win_ref, wgrp_ref, scale_ref, wout_ref, h_ref, ga_ref, gb_ref,
                 ho_ref, uo_ref, *, seq_tiles):
    i = pl.program_id(0)
    tm = u_ref.shape[0]
    halo = BF16_SUBLANES
    win = win_ref[...]
    z = _dot(u_ref[...], win)
    zh = jnp.where(i % seq_tiles == 0, 0.0, _dot(uh_ref[...], win))
    tpos = (i % seq_tiles) * tm + lax.broadcasted_iota(jnp.int32, (tm, 1), 0)
    parts = []
    for g, w in enumerate(POOL_WINDOWS):
        cols = slice(g * POOL_GROUP_DIM, (g + 1) * POOL_GROUP_DIM)
        x = jnp.concatenate([zh[:, cols], z[:, cols]], axis=0)
        s = x
        k = 1
        while k < w:
            s = s + pltpu.roll(s, k, 0)
            k *= 2
        cnt = jnp.minimum(tpos + 1, w).astype(F32)
        p = s[halo:] / cnt - x[halo:]
        mg = _dot(p.astype(BF16), wgrp_ref[g]) * scale_ref[:, cols]
        parts.append(mg.astype(BF16))
    m = jnp.concatenate(parts, axis=1)
    _residual_update(_dot(m, wout_ref[...]), h_ref, ga_ref, gb_ref, ho_ref, uo_ref)


def _pool_mixer(u, h, w_in, w_grp, scale, w_out, ga, gb, seq):
    n = u.shape[0]
    tm = MIX_ROW_TILE
    row = lambda i: (i, 0)
    return pl.pallas_call(
        functools.partial(_pool_kernel, seq_tiles=seq // tm),
        grid=(n // tm,),
        in_specs=[pl.BlockSpec((tm, D_MODEL), row),
                  pl.BlockSpec((BF16_SUBLANES, D_MODEL), _halo_index(tm)),
                  _const_spec((D_MODEL, D_MODEL)),
                  _const_spec(w_grp.shape),
                  _const_spec((1, D_MODEL)),
                  _const_spec((D_MODEL, D_MODEL)),
                  pl.BlockSpec((tm, D_MODEL), row),
                  _const_spec((1, D_MODEL)),
                  _const_spec((1, D_MODEL))],
        out_specs=[pl.BlockSpec((tm, D_MODEL), row), pl.BlockSpec((tm, D_MODEL), row)],
        out_shape=[jax.ShapeDtypeStruct((n, D_MODEL), F32),
                   jax.ShapeDtypeStruct((n, D_MODEL), BF16)],
        compiler_params=_cparams("parallel"),
        name="pool_mixer",
    )(u, u, w_in, w_grp, scale, w_out, h, ga, gb)


def _gelu_tanh(x):
    c = math.sqrt(2.0 / math.pi)
    return 0.5 * x * (1.0 + jnp.tanh(c * (x + 0.044715 * (x * x * x))))


def _sgu_kernel(u_ref, win_ref, lng_ref, lnb_ref, ws_ref, bst_ref, wout_ref, h_ref, ga_ref, gb_ref,
                ho_ref, uo_ref, gated_ref):
    tm = u_ref.shape[0]
    y = _gelu_tanh(_dot(u_ref[...], win_ref[...]))
    uu = y[:, :D_MODEL]
    v = y[:, D_MODEL:]
    mu = jnp.mean(v, axis=-1, keepdims=True)
    vc = v - mu
    var = jnp.mean(vc * vc, axis=-1, keepdims=True)
    vn = (vc * lax.rsqrt(var + LN_EPS) * lng_ref[...] + lnb_ref[...]).astype(BF16)
    t_idx = lax.broadcasted_iota(jnp.int32, (SGU_CHUNK, SGU_CHUNK), 0)
    s_idx = lax.broadcasted_iota(jnp.int32, (SGU_CHUNK, SGU_CHUNK), 1)
    causal = s_idx <= t_idx
    bst = bst_ref[...]
    for g in range(SGU_GROUPS):
        cols = slice(g * SGU_GROUP_DIM, (g + 1) * SGU_GROUP_DIM)
        ws = jnp.where(causal, ws_ref[g], 0.0).astype(BF16)
        bias = bst[:, g:g + 1]
        for ci in range(tm // SGU_CHUNK):
            rows = slice(ci * SGU_CHUNK, (ci + 1) * SGU_CHUNK)
            mixed = _dot(ws, vn[rows, cols]) + bias
            gated_ref[rows, cols] = (uu[rows, cols] * mixed).astype(BF16)
    _residual_update(_dot(gated_ref[...], wout_ref[...]), h_ref, ga_ref, gb_ref, ho_ref, uo_ref)


def _sgu_mixer(u, h, w_in, ln_g, ln_b, w_s, b_s_t, w_out, ga, gb):
    n = u.shape[0]
    tm = MIX_ROW_TILE
    row = lambda i: (i, 0)
    return pl.pallas_call(
        _sgu_kernel,
        grid=(n // tm,),
        in_specs=[pl.BlockSpec((tm, D_MODEL), row),
                  _const_spec((D_MODEL, 2 * D_MODEL)),
                  _const_spec((1, D_MODEL)),
                  _const_spec((1, D_MODEL)),
                  _const_spec(w_s.shape),
                  _const_spec(b_s_t.shape),
                  _const_spec((D_MODEL, D_MODEL)),
                  pl.BlockSpec((tm, D_MODEL), row),
                  _const_spec((1, D_MODEL)),
                  _const_spec((1, D_MODEL))],
        out_specs=[pl.BlockSpec((tm, D_MODEL), row), pl.BlockSpec((tm, D_MODEL), row)],
        out_shape=[jax.ShapeDtypeStruct((n, D_MODEL), F32),
                   jax.ShapeDtypeStruct((n, D_MODEL), BF16)],
        scratch_shapes=[pltpu.VMEM((tm, D_MODEL), BF16)],
        compiler_params=_cparams("parallel"),
        name="sgu_mixer",
    )(u, w_in, ln_g, ln_b, w_s, b_s_t, w_out, h, ga, gb)


def _nsa_proj_kernel(u_ref, w_ref, bias_ref, cos_ref, sin_ref, o_ref, *, rope_chunks, gate_chunks):
    j = pl.program_id(1)
    acc = _dot(u_ref[...], w_ref[...])
    tn = acc.shape[1]

    def is_in(chunks):
        hit = j == chunks[0]
        for c in chunks[1:]:
            hit = hit | (j == c)
        return hit

    rope = is_in(rope_chunks)
    plain = jnp.logical_not(rope)
    if gate_chunks:
        gate = is_in(gate_chunks)
        plain = plain & jnp.logical_not(gate)

        @pl.when(gate)
        def _():
            o_ref[...] = (1.0 / (1.0 + jnp.exp(-(acc + bias_ref[...])))).astype(o_ref.dtype)

    @pl.when(rope)
    def _():
        cos = cos_ref[...]
        sin = sin_ref[...]
        for hh in range(tn // HEAD_DIM):
            cols = slice(hh * HEAD_DIM, (hh + 1) * HEAD_DIM)
            x = acc[:, cols]
            o_ref[:, cols] = (x * cos + pltpu.roll(x, HEAD_DIM // 2, 1) * sin).astype(o_ref.dtype)

    @pl.when(plain)
    def _():
        o_ref[...] = acc.astype(o_ref.dtype)


def _nsa_proj(u, w, bias, cos, sin, *, tn, out_dtype, rope_chunks, gate_chunks, seq):
    n = u.shape[0]
    ncols = w.shape[1]
    tm = ROW_TILE
    seq_tiles = seq // tm
    return pl.pallas_call(
        functools.partial(_nsa_proj_kernel, rope_chunks=rope_chunks, gate_chunks=gate_chunks),
        grid=(n // tm, ncols // tn),
        in_specs=[pl.BlockSpec((tm, D_MODEL), lambda i, j: (i, 0)),
                  pl.BlockSpec((D_MODEL, tn), lambda i, j: (0, j)),
                  pl.BlockSpec((1, tn), lambda i, j: (0, j)),
                  pl.BlockSpec((tm, HEAD_DIM), lambda i, j: (i % seq_tiles, 0)),
                  pl.BlockSpec((tm, HEAD_DIM), lambda i, j: (i % seq_tiles, 0))],
        out_specs=pl.BlockSpec((tm, tn), lambda i, j: (i, j)),
        out_shape=jax.ShapeDtypeStruct((n, ncols), out_dtype),
        compiler_params=_cparams("parallel", "arbitrary"),
        name="nsa_proj",
    )(u, w, bias, cos, sin)


def _compress_kernel(a_ref, pe_ref, w1_ref, b1_ref, w2_ref, b2_ref, o_ref):
    nb = a_ref.shape[0] // CMP_STRIDE
    top = jnp.zeros((nb, CMP_HIDDEN), F32)
    bot = jnp.zeros((nb, CMP_HIDDEN), F32)
    for p in range(CMP_STRIDE):
        xp = a_ref[pl.ds(p, nb, stride=CMP_STRIDE), :]
        lo = slice(p * HEAD_DIM, (p + 1) * HEAD_DIM)
        hi = slice((CMP_STRIDE + p) * HEAD_DIM, (CMP_STRIDE + p + 1) * HEAD_DIM)
        top += _dot((xp + pe_ref[p:p + 1, :]).astype(BF16), w1_ref[lo, :].astype(BF16))
        bot += _dot((xp + pe_ref[CMP_STRIDE + p:CMP_STRIDE + p + 1, :]).astype(BF16),
                    w1_ref[hi, :].astype(BF16))
    hid = _gelu_tanh(top + pltpu.roll(bot, nb - 1, 0) + b1_ref[...])
    o_ref[...] = (_dot(hid.astype(BF16), w2_ref[...].astype(BF16)) + b2_ref[...]).astype(o_ref.dtype)


def _compress(cv, pe, w1, b1, w2, b2, batch, seq):
    nb = seq // CMP_STRIDE
    return pl.pallas_call(
        _compress_kernel,
        grid=(batch, 2, N_KV_HEADS),
        in_specs=[pl.BlockSpec((seq, HEAD_DIM), lambda b, s, g: (b, s * N_KV_HEADS + g)),
                  pl.BlockSpec((None, CMP_BLOCK, HEAD_DIM), lambda b, s, g: (s, 0, 0)),
                  pl.BlockSpec((None, CMP_BLOCK * HEAD_DIM, CMP_HIDDEN), lambda b, s, g: (s, 0, 0)),
                  pl.BlockSpec((None, 1, CMP_HIDDEN), lambda b, s, g: (s, 0, 0)),
                  pl.BlockSpec((None, CMP_HIDDEN, HEAD_DIM), lambda b, s, g: (s, 0, 0)),
                  pl.BlockSpec((None, 1, HEAD_DIM), lambda b, s, g: (s, 0, 0))],
        out_specs=pl.BlockSpec((None, None, None, nb, HEAD_DIM), lambda b, s, g: (b, s, g, 0, 0)),
        out_shape=jax.ShapeDtypeStruct((batch, 2, N_KV_HEADS, nb, HEAD_DIM), BF16),
        compiler_params=_cparams("parallel", "parallel", "parallel"),
        name="nsa_compress",
    )(cv, pe, w1, b1, w2, b2)


def _softmax_step(s, v, m, l, acc):
    m_new = jnp.maximum(m, jnp.max(s, axis=-1, keepdims=True))
    alpha = jnp.exp(m - m_new)
    p = jnp.exp(s - m_new)
    l_new = alpha * l + jnp.sum(p, axis=-1, keepdims=True)
    acc_new = alpha * acc + _dot(p.astype(BF16), v)
    return m_new, l_new, acc_new


def _nsa_attn_kernel(q_ref, ks_ref, vs_ref, kw_ref, vw_ref, kc_ref, vc_ref, gate_ref, ov_ref, ex_ref,
                     o_ref, *, n_cmp):
    i = pl.program_id(2)
    tq = ATT_TILE
    rows = GQA_REP * tq
    scale = HEAD_DIM ** -0.5
    q = q_ref[...]
    qs = jnp.concatenate([q[:, r * HEAD_DIM:(r + 1) * HEAD_DIM] for r in range(GQA_REP)], axis=0)
    t1 = i * tq + lax.broadcasted_iota(jnp.int32, (tq, LANES), 0)
    t4 = jnp.concatenate([t1] * GQA_REP, axis=0)
    lane1 = lax.broadcasted_iota(jnp.int32, (tq, LANES), 1)
    lane4 = lax.broadcasted_iota(jnp.int32, (rows, LANES), 1)

    sc = _dot_nt(qs, kc_ref[...]) * scale
    ok_c = (lane4 * CMP_STRIDE + (CMP_BLOCK - 1) <= t4) & (lane4 < n_cmp)
    sc = jnp.where(ok_c, sc, NEG)
    e = jnp.exp(sc - jnp.max(sc, axis=-1, keepdims=True))
    p_c = e / jnp.sum(e, axis=-1, keepdims=True)
    p_c = p_c * (t4 >= CMP_BLOCK - 1).astype(F32)
    o_c = _dot(p_c.astype(BF16), vc_ref[...])

    p_sum = p_c[0:tq]
    for r in range(1, GQA_REP):
        p_sum = p_sum + p_c[r * tq:(r + 1) * tq]
    p_hi = p_sum.astype(BF16)
    p_lo = (p_sum - p_hi.astype(F32)).astype(BF16)
    ov = ov_ref[...]
    imp = _dot(p_hi, ov) + _dot(p_lo, ov)
    n_slc = LANES // 4
    blk = lane1 & (n_slc - 1)
    cur = t1 >> int(math.log2(SLC_BLOCK))
    forced = (blk == 0) | (blk == cur) | (blk == cur - 1)
    imp = jnp.where(forced, BIG, imp)
    imp = jnp.where(blk <= cur, imp, NEG)
    rank = jnp.zeros((tq, LANES), jnp.int32)
    for d in range(1, n_slc):
        other = pltpu.roll(imp, d, 1)
        ahead = (other > imp) | ((other == imp) & (blk >= d))
        rank = rank + ahead.astype(jnp.int32)
    sel = jnp.where((rank < SLC_TOPK) & (blk <= cur), 1.0, 0.0).astype(BF16)

    m0 = jnp.full((rows, 1), NEG, F32)
    l0 = jnp.zeros((rows, 1), F32)
    a0 = jnp.zeros((rows, HEAD_DIM), F32)

    def slc_body(kt, carry):
        start = pl.multiple_of(kt * tq, tq)
        k = ks_ref[pl.ds(start, tq), :]
        v = vs_ref[pl.ds(start, tq), :]
        s = _dot_nt(qs, k) * scale
        picked = _dot(sel, ex_ref[kt])
        picked = jnp.concatenate([picked] * GQA_REP, axis=0)
        kpos = kt * tq + lane4
        s = jnp.where((picked > 0.5) & (kpos <= t4), s, NEG)
        return _softmax_step(s, v, *carry)

    _, l_s, a_s = lax.fori_loop(0, i + 1, slc_body, (m0, l0, a0))
    o_s = a_s / l_s

    def win_body(kt, carry):
        start = pl.multiple_of(kt * tq, tq)
        k = kw_ref[pl.ds(start, tq), :]
        v = vw_ref[pl.ds(start, tq), :]
        s = _dot_nt(qs, k) * scale
        diff = t4 - (kt * tq + lane4)
        s = jnp.where((diff >= 0) & (diff < WIN), s, NEG)
        return _softmax_step(s, v, *carry)

    _, l_w, a_w = lax.fori_loop(jnp.maximum(i - WIN // tq, 0), i + 1, win_body, (m0, l0, a0))
    o_w = a_w / l_w

    gates = gate_ref[...]
    for r in range(GQA_REP):
        rs = slice(r * tq, (r + 1) * tq)
        g_c = gates[:, r * N_GATES + 0:r * N_GATES + 1]
        g_s = gates[:, r * N_GATES + 1:r * N_GATES + 2]
        g_w = gates[:, r * N_GATES + 2:r * N_GATES + 3]
        o = g_c * o_c[rs] + g_s * o_s[rs] + g_w * o_w[rs]
        o_ref[:, r * HEAD_DIM:(r + 1) * HEAD_DIM] = o.astype(o_ref.dtype)


def _nsa_attention(qkv, cmp, cvg, ov, ex, batch, seq):
    n = qkv.shape[0]
    tq = ATT_TILE
    qt = seq // tq
    gw = GQA_REP * HEAD_DIM
    kv0 = NSA_Q_DIM // HEAD_DIM
    gate0 = 2 * N_KV_HEADS

    def kv_spec(which):
        return pl.BlockSpec((seq, HEAD_DIM), lambda b, g, i: (b, kv0 + which * N_KV_HEADS + g))

    def cmp_spec(which):
        return pl.BlockSpec((None, None, None, seq // CMP_STRIDE, HEAD_DIM),
                            lambda b, g, i: (b, which, g, 0, 0))

    return pl.pallas_call(
        functools.partial(_nsa_attn_kernel, n_cmp=seq // CMP_STRIDE - CMP_BLOCK // CMP_STRIDE + 1),
        grid=(batch, N_KV_HEADS, qt),
        in_specs=[pl.BlockSpec((tq, gw), lambda b, g, i: (b * qt + i, g)),
                  kv_spec(0), kv_spec(1), kv_spec(2), kv_spec(3),
                  cmp_spec(0), cmp_spec(1),
                  pl.BlockSpec((tq, LANES), lambda b, g, i: (b * qt + i, gate0 + g)),
                  pl.BlockSpec(ov.shape, lambda b, g, i: (0, 0)),
                  pl.BlockSpec(ex.shape, lambda b, g, i: (0, 0, 0))],
        out_specs=pl.BlockSpec((tq, gw), lambda b, g, i: (b * qt + i, g)),
        out_shape=jax.ShapeDtypeStruct((n, NSA_Q_DIM), BF16),
        compiler_params=_cparams("parallel", "parallel", "arbitrary"),
        name="nsa_attention",
    )(qkv, qkv, qkv, qkv, qkv, cmp, cmp, cvg, ov, ex)


def _rope_tables(seq):
    half = HEAD_DIM // 2
    inv = 1.0 / (ROPE_THETA ** (jnp.arange(half, dtype=F32) / half))
    ang = jnp.arange(seq, dtype=F32)[:, None] * inv[None, :]
    cos = jnp.cos(ang)
    sin = jnp.sin(ang)
    return jnp.concatenate([cos, cos], axis=1), jnp.concatenate([-sin, sin], axis=1)


def _selection_constants(seq):
    n_cmp = seq // CMP_STRIDE - CMP_BLOCK // CMP_STRIDE + 1
    n_slc = seq // SLC_BLOCK
    assert 4 * n_slc == LANES and seq // CMP_STRIDE == LANES
    ci = np.arange(LANES)[:, None]
    sj = (np.arange(LANES) % n_slc)[None, :]
    ov = ((ci * CMP_STRIDE <= (sj + 1) * SLC_BLOCK - 1)
          & (ci * CMP_STRIDE + CMP_BLOCK - 1 >= sj * SLC_BLOCK) & (ci < n_cmp))
    key = np.arange(seq)
    ex = (np.arange(LANES)[:, None] == (key // SLC_BLOCK)[None, :])
    ex = ex.reshape(LANES, seq // ATT_TILE, ATT_TILE).transpose(1, 0, 2)
    return jnp.asarray(ov, BF16), jnp.asarray(ex, BF16)


def _nsa_mixer(u, h, w_in, gate_b, cmp_pe, cmp_w1, cmp_b1, cmp_w2, cmp_b2, w_out, ga, gb, batch, seq):
    q0 = NSA_Q_DIM
    kvd = NSA_KV_DIM

    def part(k):
        return w_in[:, q0 + k * kvd:q0 + (k + 1) * kvd]

    w_a = jnp.concatenate([w_in[:, :q0], part(2), part(3), part(4), part(5)], axis=1).astype(BF16)
    per_group = GQA_REP * N_GATES
    w_g = w_in[:, q0 + 6 * kvd:].reshape(D_MODEL, N_KV_HEADS, per_group)
    w_g = jnp.pad(w_g, ((0, 0), (0, 0), (0, LANES - per_group))).reshape(D_MODEL, N_KV_HEADS * LANES)
    b_g = jnp.pad(gate_b.reshape(N_KV_HEADS, per_group), ((0, 0), (0, LANES - per_group)))
    w_b = jnp.concatenate([part(0), part(1), w_g], axis=1).astype(BF16)
    bias_b = jnp.concatenate([jnp.zeros((2 * kvd,), F32), b_g.reshape(-1)])[None, :]
    cos, sin = _rope_tables(seq)

    tn_a = 512
    qkv = _nsa_proj(u, w_a, jnp.zeros((1, w_a.shape[1]), F32), cos, sin, tn=tn_a, out_dtype=BF16,
                    rope_chunks=tuple(range(q0 // tn_a)) + (q0 // tn_a, q0 // tn_a + 2),
                    gate_chunks=(), seq=seq)
    cvg = _nsa_proj(u, w_b, bias_b, cos, sin, tn=LANES, out_dtype=F32,
                    rope_chunks=tuple(range(N_KV_HEADS)),
                    gate_chunks=tuple(range(2 * N_KV_HEADS, 3 * N_KV_HEADS)), seq=seq)
    cmp = _compress(cvg, cmp_pe, cmp_w1, cmp_b1[:, None, :], cmp_w2, cmp_b2[:, None, :], batch, seq)
    ov, ex = _selection_constants(seq)
    o = _nsa_attention(qkv, cmp, cvg, ov, ex, batch, seq)
    return _out_proj(o, w_out.astype(BF16), h, ga, gb)


def kernel(x, norm_g, ffn_w_up, ffn_conv_w, ffn_conv_b, ffn_w_down, pool_w_in, pool_w_grp, pool_scale,
           pool_w_out, sgu_w_in, sgu_ln_g, sgu_ln_b, sgu_w_s, sgu_b_s, sgu_w_out, nsa_w_in, nsa_gate_b,
           nsa_cmp_pe, nsa_cmp_w1, nsa_cmp_b1, nsa_cmp_w2, nsa_cmp_b2, nsa_w_out):
    batch, seq, d = x.shape
    h = x.reshape(batch * seq, d)

    def gain(i, k):
        return norm_g[i, k][None, :]

    u = _first_norm(h, gain(0, 0))
    for i in range(DEPTH):
        kind, j = i % 3, i // 3
        ga, gb = gain(i, 1), gain(i, 2)
        if kind == 0:
            h, u = _pool_mixer(u, h, pool_w_in[j].astype(BF16), pool_w_grp[j].astype(BF16),
                               pool_scale[j][None, :], pool_w_out[j].astype(BF16), ga, gb, seq)
        elif kind == 1:
            h, u = _sgu_mixer(u, h, sgu_w_in[j].astype(BF16), sgu_ln_g[j][None, :], sgu_ln_b[j][None, :],
                              sgu_w_s[j], sgu_b_s[j].T, sgu_w_out[j].astype(BF16), ga, gb)
        else:
            h, u = _nsa_mixer(u, h, nsa_w_in[j], nsa_gate_b[j], nsa_cmp_pe[j], nsa_cmp_w1[j],
                              nsa_cmp_b1[j], nsa_cmp_w2[j], nsa_cmp_b2[j], nsa_w_out[j], ga, gb,
                              batch, seq)
        g_next = gain(i + 1, 0) if i + 1 < DEPTH else gain(i, 3)
        h, u = _conv_ffn(u, h, ffn_w_up[i].astype(BF16), ffn_conv_w[i], ffn_conv_b[i][None, :],
                         ffn_w_down[i].astype(BF16), gain(i, 3), g_next, seq)
    return h.reshape(batch, seq, d)
```

```python
import functools
import math

import jax
import jax.numpy as jnp
import numpy as np
from jax import lax
from jax.experimental import pallas as pl
from jax.experimental.pallas import tpu as pltpu

F32 = jnp.float32
BF16 = jnp.bfloat16

D_MODEL = 2048
DEPTH = 4
RMS_EPS = 1e-6
LN_EPS = 1e-5
NEG = -1e30
BIG = 1e30

FFN_DIM = 5632
POOL_WINDOWS = (2, 4, 8, 16)
POOL_GROUP_DIM = D_MODEL // len(POOL_WINDOWS)
SGU_CHUNK = 128
SGU_GROUPS = 16
SGU_GROUP_DIM = D_MODEL // SGU_GROUPS

HEAD_DIM = 128
N_HEADS = 16
N_KV_HEADS = 4
GQA_REP = N_HEADS // N_KV_HEADS
ROPE_THETA = 10000.0
CMP_BLOCK = 32
CMP_STRIDE = 16
CMP_HIDDEN = 2 * HEAD_DIM
SLC_BLOCK = 64
SLC_TOPK = 16
WIN = 512
NSA_Q_DIM = N_HEADS * HEAD_DIM
NSA_KV_DIM = N_KV_HEADS * HEAD_DIM
N_GATES = 3

LANES = 128
BF16_SUBLANES = 16
VMEM_LIMIT = 56 * 1024 * 1024

ROW_TILE = 512
MIX_ROW_TILE = 256
FFN_COL_TILE = 512
ATT_TILE = 128


def _cparams(*sem):
    return pltpu.CompilerParams(dimension_semantics=sem, vmem_limit_bytes=VMEM_LIMIT)


def _const_spec(shape):
    nd = len(shape)
    return pl.BlockSpec(shape, lambda *_: (0,) * nd, pipeline_mode=pl.Buffered(1))


def _rms(x):
    return x * lax.rsqrt(jnp.mean(x * x, axis=-1, keepdims=True) + RMS_EPS)


def _residual_update(m, h_ref, ga_ref, gb_ref, ho_ref, uo_ref):
    hn = h_ref[...] + _rms(m) * ga_ref[...]
    ho_ref[...] = hn
    uo_ref[...] = (_rms(hn) * gb_ref[...]).astype(BF16)


def _dot(a, b):
    return jnp.dot(a, b, preferred_element_type=F32)


def _halo_index(tm):
    blocks = tm // BF16_SUBLANES
    return lambda i, *_: (jnp.maximum(i * blocks - 1, 0), 0)


def _norm_kernel(h_ref, g_ref, u_ref):
    u_ref[...] = (_rms(h_ref[...]) * g_ref[...]).astype(BF16)


def _first_norm(h, g):
    n = h.shape[0]
    return pl.pallas_call(
        _norm_kernel,
        grid=(n // ROW_TILE,),
        in_specs=[pl.BlockSpec((ROW_TILE, D_MODEL), lambda i: (i, 0)),
                  pl.BlockSpec((1, D_MODEL), lambda i: (0, 0))],
        out_specs=pl.BlockSpec((ROW_TILE, D_MODEL), lambda i: (i, 0)),
        out_shape=jax.ShapeDtypeStruct((n, D_MODEL), BF16),
        compiler_params=_cparams("parallel"),
        name="first_norm",
    )(h, g)


def _out_proj_kernel(x_ref, w_ref, h_ref, ga_ref, gb_ref, ho_ref, uo_ref):
    _residual_update(_dot(x_ref[...], w_ref[...]), h_ref, ga_ref, gb_ref, ho_ref, uo_ref)


def _out_proj(xin, w, h, ga, gb):
    n, k = xin.shape
    tm = ROW_TILE
    row = lambda i: (i, 0)
    return pl.pallas_call(
        _out_proj_kernel,
        grid=(n // tm,),
        in_specs=[pl.BlockSpec((tm, k), row), _const_spec((k, D_MODEL)),
                  pl.BlockSpec((tm, D_MODEL), row),
                  pl.BlockSpec((1, D_MODEL), lambda i: (0, 0)),
                  pl.BlockSpec((1, D_MODEL), lambda i: (0, 0))],
        out_specs=[pl.BlockSpec((tm, D_MODEL), row), pl.BlockSpec((tm, D_MODEL), row)],
        out_shape=[jax.ShapeDtypeStruct((n, D_MODEL), F32),
                   jax.ShapeDtypeStruct((n, D_MODEL), BF16)],
        compiler_params=_cparams("parallel"),
        name="out_proj",
    )(xin, w, h, ga, gb)


def _ffn_kernel(u_ref, uh_ref, wa_ref, wb_ref, cw_ref, cb_ref, wd_ref, h_ref, ga_ref, gb_ref,
                ho_ref, uo_ref, acc_ref, *, seq_tiles):
    i = pl.program_id(0)
    c = pl.program_id(1)

    @pl.when(c == 0)
    def _():
        acc_ref[...] = jnp.zeros_like(acc_ref)

    u = u_ref[...]
    wa = wa_ref[...]
    a = _dot(u, wa)
    b = _dot(u, wb_ref[...])
    ah = _dot(uh_ref[...], wa)
    ah = jnp.where(i % seq_tiles == 0, 0.0, ah)
    row = lax.broadcasted_iota(jnp.int32, a.shape, 0)
    last1 = ah[BF16_SUBLANES - 1:BF16_SUBLANES]
    last2 = ah[BF16_SUBLANES - 2:BF16_SUBLANES - 1]
    p1 = jnp.where(row == 0, last1, pltpu.roll(a, 1, 0))
    p2 = jnp.where(row == 0, last2, jnp.where(row == 1, last1, pltpu.roll(a, 2, 0)))
    cw = cw_ref[...]
    y = cw[0:1] * p2 + cw[1:2] * p1 + cw[2:3] * a + cb_ref[...]
    gated = y / (1.0 + jnp.exp(-y)) * b
    acc_ref[...] += _dot(gated.astype(BF16), wd_ref[...])

    @pl.when(c == pl.num_programs(1) - 1)
    def _():
        _residual_update(acc_ref[...], h_ref, ga_ref, gb_ref, ho_ref, uo_ref)


def _conv_ffn(u, h, w_up, conv_w, conv_b, w_down, ga, gb, seq):
    n = u.shape[0]
    tm, tf = ROW_TILE, FFN_COL_TILE
    n_chunks = FFN_DIM // tf
    row = lambda i, c: (i, 0)
    vec = lambda i, c: (0, 0)
    return pl.pallas_call(
        functools.partial(_ffn_kernel, seq_tiles=seq // tm),
        grid=(n // tm, n_chunks),
        in_specs=[pl.BlockSpec((tm, D_MODEL), row),
                  pl.BlockSpec((BF16_SUBLANES, D_MODEL), _halo_index(tm)),
                  pl.BlockSpec((D_MODEL, tf), lambda i, c: (0, c)),
                  pl.BlockSpec((D_MODEL, tf), lambda i, c: (0, c + n_chunks)),
                  pl.BlockSpec((3, tf), lambda i, c: (0, c)),
                  pl.BlockSpec((1, tf), lambda i, c: (0, c)),
                  pl.BlockSpec((tf, D_MODEL), lambda i, c: (c, 0)),
                  pl.BlockSpec((tm, D_MODEL), row),
                  pl.BlockSpec((1, D_MODEL), vec),
                  pl.BlockSpec((1, D_MODEL), vec)],
        out_specs=[pl.BlockSpec((tm, D_MODEL), row), pl.BlockSpec((tm, D_MODEL), row)],
        out_shape=[jax.ShapeDtypeStruct((n, D_MODEL), F32),
                   jax.ShapeDtypeStruct((n, D_MODEL), BF16)],
        scratch_shapes=[pltpu.VMEM((tm, D_MODEL), F32)],
        compiler_params=_cparams("parallel", "arbitrary"),
        name="conv_ffn",
    )(u, u, w_up, w_up, conv_w, conv_b, w_down, h, ga, gb)


def _pool_kernel(u_ref, uh_ref, win_ref, wgrp_ref, scale_ref, wout_ref, h_ref, ga_ref, gb_ref,
                 ho_ref, uo_ref, *, seq_tiles):
    i = pl.program_id(0)
    tm = u_ref.shape[0]
    halo = BF16_SUBLANES
    win = win_ref[...]
    z = _dot(u_ref[...], win)
    zh = jnp.where(i % seq_tiles == 0, 0.0, _dot(uh_ref[...], win))
    tpos = (i % seq_tiles) * tm + lax.broadcasted_iota(jnp.int32, (tm, 1), 0)
    parts = []
    for g, w in enumerate(POOL_WINDOWS):
        cols = slice(g * POOL_GROUP_DIM, (g + 1) * POOL_GROUP_DIM)
        x = jnp.concatenate([zh[:, cols], z[:, cols]], axis=0)
        s = x
        k = 1
        while k < w:
            s = s + pltpu.roll(s, k, 0)
            k *= 2
        cnt = jnp.minimum(tpos + 1, w).astype(F32)
        p = s[halo:] / cnt - x[halo:]
        mg = _dot(p.astype(BF16), wgrp_ref[g]) * scale_ref[:, cols]
        parts.append(mg.astype(BF16))
    m = jnp.concatenate(parts, axis=1)
    _residual_update(_dot(m, wout_ref[...]), h_ref, ga_ref, gb_ref, ho_ref, uo_ref)


def _pool_mixer(u, h, w_in, w_grp, scale, w_out, ga, gb, seq):
    n = u.shape[0]
    tm = MIX_ROW_TILE
    row = lambda i: (i, 0)
    return pl.pallas_call(
        functools.partial(_pool_kernel, seq_tiles=seq // tm),
        grid=(n // tm,),
        in_specs=[pl.BlockSpec((tm, D_MODEL), row),
                  pl.BlockSpec((BF16_SUBLANES, D_MODEL), _halo_index(tm)),
                  _const_spec((D_MODEL, D_MODEL)),
                  _const_spec(w_grp.shape),
                  _const_spec((1, D_MODEL)),
                  _const_spec((D_MODEL, D_MODEL)),
                  pl.BlockSpec((tm, D_MODEL), row),
                  _const_spec((1, D_MODEL)),
                  _const_spec((1, D_MODEL))],
        out_specs=[pl.BlockSpec((tm, D_MODEL), row), pl.BlockSpec((tm, D_MODEL), row)],
        out_shape=[jax.ShapeDtypeStruct((n, D_MODEL), F32),
                   jax.ShapeDtypeStruct((n, D_MODEL), BF16)],
        compiler_params=_cparams("parallel"),
        name="pool_mixer",
    )(u, u, w_in, w_grp, scale, w_out, h, ga, gb)


def _gelu_tanh(x):
    c = math.sqrt(2.0 / math.pi)
    return 0.5 * x * (1.0 + jnp.tanh(c * (x + 0.044715 * (x * x * x))))


def _sgu_kernel(u_ref, win_ref, lng_ref, lnb_ref, ws_ref, bst_ref, wout_ref, h_ref, ga_ref, gb_ref,
                ho_ref, uo_ref, gated_ref):
    tm = u_ref.shape[0]
    y = _gelu_tanh(_dot(u_ref[...], win_ref[...]))
    uu = y[:, :D_MODEL]
    v = y[:, D_MODEL:]
    mu = jnp.mean(v, axis=-1, keepdims=True)
    vc = v - mu
    var = jnp.mean(vc * vc, axis=-1, keepdims=True)
    vn = (vc * lax.rsqrt(var + LN_EPS) * lng_ref[...] + lnb_ref[...]).astype(BF16)
    t_idx = lax.broadcasted_iota(jnp.int32, (SGU_CHUNK, SGU_CHUNK), 0)
    s_idx = lax.broadcasted_iota(jnp.int32, (SGU_CHUNK, SGU_CHUNK), 1)
    causal = s_idx <= t_idx
    bst = bst_ref[...]
    for g in range(SGU_GROUPS):
        cols = slice(g * SGU_GROUP_DIM, (g + 1) * SGU_GROUP_DIM)
        ws = jnp.where(causal, ws_ref[g], 0.0).astype(BF16)
        bias = bst[:, g:g + 1]
        for ci in range(tm // SGU_CHUNK):
            rows = slice(ci * SGU_CHUNK, (ci + 1) * SGU_CHUNK)
            mixed = _dot(ws, vn[rows, cols]) + bias
            gated_ref[rows, cols] = (uu[rows, cols] * mixed).astype(BF16)
    _residual_update(_dot(gated_ref[...], wout_ref[...]), h_ref, ga_ref, gb_ref, ho_ref, uo_ref)


def _sgu_mixer(u, h, w_in, ln_g, ln_b, w_s, b_s_t, w_out, ga, gb):
    n = u.shape[0]
    tm = MIX_ROW_TILE
    row = lambda i: (i, 0)
    return pl.pallas_call(
        _sgu_kernel,
        grid=(n // tm,),
        in_specs=[pl.BlockSpec((tm, D_MODEL), row),
                  _const_spec((D_MODEL, 2 * D_MODEL)),
                  _const_spec((1, D_MODEL)),
                  _const_spec((1, D_MODEL)),
                  _const_spec(w_s.shape),
                  _const_spec(b_s_t.shape),
                  _const_spec((D_MODEL, D_MODEL)),
                  pl.BlockSpec((tm, D_MODEL), row),
                  _const_spec((1, D_MODEL)),
                  _const_spec((1, D_MODEL))],
        out_specs=[pl.BlockSpec((tm, D_MODEL), row), pl.BlockSpec((tm, D_MODEL), row)],
        out_shape=[jax.ShapeDtypeStruct((n, D_MODEL), F32),
                   jax.ShapeDtypeStruct((n, D_MODEL), BF16)],
        scratch_shapes=[pltpu.VMEM((tm, D_MODEL), BF16)],
        compiler_params=_cparams("parallel"),
        name="sgu_mixer",
    )(u, w_in, ln_g, ln_b, w_s, b_s_t, w_out, h, ga, gb)


def _nsa_proj_kernel(u_ref, w_ref, bias_ref, cos_ref, sin_ref, o_ref, *, rope_chunks, gate_chunks):
    j = pl.program_id(1)
    acc = _dot(u_ref[...], w_ref[...])
    tn = acc.shape[1]

    def is_in(chunks):
        hit = j == chunks[0]
        for c in chunks[1:]:
            hit = hit | (j == c)
        return hit

    rope = is_in(rope_chunks)
    plain = jnp.logical_not(rope)
    if gate_chunks:
        gate = is_in(gate_chunks)
        plain = plain & jnp.logical_not(gate)

        @pl.when(gate)
        def _():
            o_ref[...] = (1.0 / (1.0 + jnp.exp(-(acc + bias_ref[...])))).astype(o_ref.dtype)

    @pl.when(rope)
    def _():
        cos = cos_ref[...]
        sin = sin_ref[...]
        for hh in range(tn // HEAD_DIM):
            cols = slice(hh * HEAD_DIM, (hh + 1) * HEAD_DIM)
            x = acc[:, cols]
            o_ref[:, cols] = (x * cos + pltpu.roll(x, HEAD_DIM // 2, 1) * sin).astype(o_ref.dtype)

    @pl.when(plain)
    def _():
        o_ref[...] = acc.astype(o_ref.dtype)


def _nsa_proj(u, w, bias, cos, sin, *, tn, out_dtype, rope_chunks, gate_chunks, seq):
    n = u.shape[0]
    ncols = w.shape[1]
    tm = ROW_TILE
    seq_tiles = seq // tm
    return pl.pallas_call(
        functools.partial(_nsa_proj_kernel, rope_chunks=rope_chunks, gate_chunks=gate_chunks),
        grid=(n // tm, ncols // tn),
        in_specs=[pl.BlockSpec((tm, D_MODEL), lambda i, j: (i, 0)),
                  pl.BlockSpec((D_MODEL, tn), lambda i, j: (0, j)),
                  pl.BlockSpec((1, tn), lambda i, j: (0, j)),
                  pl.BlockSpec((tm, HEAD_DIM), lambda i, j: (i % seq_tiles, 0)),
                  pl.BlockSpec((tm, HEAD_DIM), lambda i, j: (i % seq_tiles, 0))],
        out_specs=pl.BlockSpec((tm, tn), lambda i, j: (i, j)),
        out_shape=jax.ShapeDtypeStruct((n, ncols), out_dtype),
        compiler_params=_cparams("parallel", "arbitrary"),
        name="nsa_proj",
    )(u, w, bias, cos, sin)


def _nsa_vt_kernel(u_ref, w_ref, o_ref):
    acc = _dot(u_ref[...], w_ref[...])
    for kt in range(acc.shape[0] // ATT_TILE):
        for g in range(N_KV_HEADS):
            tile = acc[kt * ATT_TILE:(kt + 1) * ATT_TILE, g * HEAD_DIM:(g + 1) * HEAD_DIM]
            o_ref[kt, g] = tile.T.astype(o_ref.dtype)


def _nsa_vt(u, w):
    n = u.shape[0]
    tm = ROW_TILE
    kt = tm // ATT_TILE
    return pl.pallas_call(
        _nsa_vt_kernel,
        grid=(n // tm, 2),
        in_specs=[pl.BlockSpec((tm, D_MODEL), lambda i, j: (i, 0)),
                  pl.BlockSpec((D_MODEL, NSA_KV_DIM), lambda i, j: (0, j))],
        out_specs=pl.BlockSpec((None, kt, N_KV_HEADS, HEAD_DIM, ATT_TILE), lambda i, j: (j, i, 0, 0, 0)),
        out_shape=jax.ShapeDtypeStruct((2, n // ATT_TILE, N_KV_HEADS, HEAD_DIM, ATT_TILE), BF16),
        compiler_params=_cparams("parallel", "arbitrary"),
        name="nsa_vt",
    )(u, w)


def _compress_kernel(a_ref, pe_ref, w1_ref, b1_ref, w2_ref, b2_ref, o_ref):
    nb = a_ref.shape[0] // CMP_STRIDE
    top = jnp.zeros((nb, CMP_HIDDEN), F32)
    bot = jnp.zeros((nb, CMP_HIDDEN), F32)
    for p in range(CMP_STRIDE):
        xp = a_ref[pl.ds(p, nb, stride=CMP_STRIDE), :]
        lo = slice(p * HEAD_DIM, (p + 1) * HEAD_DIM)
        hi = slice((CMP_STRIDE + p) * HEAD_DIM, (CMP_STRIDE + p + 1) * HEAD_DIM)
        top += _dot((xp + pe_ref[p:p + 1, :]).astype(BF16), w1_ref[lo, :].astype(BF16))
        bot += _dot((xp + pe_ref[CMP_STRIDE + p:CMP_STRIDE + p + 1, :]).astype(BF16),
                    w1_ref[hi, :].astype(BF16))
    hid = _gelu_tanh(top + pltpu.roll(bot, nb - 1, 0) + b1_ref[...])
    res = _dot(hid.astype(BF16), w2_ref[...].astype(BF16)) + b2_ref[...]

    @pl.when(pl.program_id(1) == 0)
    def _():
        o_ref[...] = res.astype(o_ref.dtype)

    @pl.when(pl.program_id(1) == 1)
    def _():
        o_ref[...] = res.T.astype(o_ref.dtype)


def _compress(cv, pe, w1, b1, w2, b2, batch, seq):
    nb = seq // CMP_STRIDE
    return pl.pallas_call(
        _compress_kernel,
        grid=(batch, 2, N_KV_HEADS),
        in_specs=[pl.BlockSpec((seq, HEAD_DIM), lambda b, s, g: (b, s * N_KV_HEADS + g)),
                  pl.BlockSpec((None, CMP_BLOCK, HEAD_DIM), lambda b, s, g: (s, 0, 0)),
                  pl.BlockSpec((None, CMP_BLOCK * HEAD_DIM, CMP_HIDDEN), lambda b, s, g: (s, 0, 0)),
                  pl.BlockSpec((None, 1, CMP_HIDDEN), lambda b, s, g: (s, 0, 0)),
                  pl.BlockSpec((None, CMP_HIDDEN, HEAD_DIM), lambda b, s, g: (s, 0, 0)),
                  pl.BlockSpec((None, 1, HEAD_DIM), lambda b, s, g: (s, 0, 0))],
        out_specs=pl.BlockSpec((None, None, None, nb, HEAD_DIM), lambda b, s, g: (b, s, g, 0, 0)),
        out_shape=jax.ShapeDtypeStruct((batch, 2, N_KV_HEADS, nb, HEAD_DIM), BF16),
        compiler_params=_cparams("parallel", "parallel", "parallel"),
        name="nsa_compress",
    )(cv, pe, w1, b1, w2, b2)


def _attn_tile(z, vt, carry):
    m, l, acc = carry
    m_new = jnp.maximum(m, jnp.max(z, axis=0, keepdims=True))
    alpha = jnp.exp(m - m_new)
    p = jnp.exp(z - m_new)
    l_new = alpha * l + jnp.sum(p, axis=0, keepdims=True)
    acc_new = alpha * acc + _dot(vt, p.astype(BF16))
    return m_new, l_new, acc_new


def _attn_tile_pair(z_a, z_b, vt_a, vt_b, carry):
    m, l, acc = carry
    nq = z_a.shape[1]
    z = jnp.concatenate([z_a, z_b], axis=1)
    m_new = jnp.maximum(m, jnp.max(z, axis=0, keepdims=True))
    alpha = jnp.exp(m - m_new)
    p = jnp.exp(z - m_new)
    l_new = alpha * l + jnp.sum(p, axis=0, keepdims=True)
    p = p.astype(BF16)
    pv = jnp.concatenate([_dot(vt_a, p[:, :nq]), _dot(vt_b, p[:, nq:])], axis=1)
    return m_new, l_new, alpha * acc + pv


def _nsa_attn_kernel(q_ref, ks_ref, kw_ref, vst_ref, vwt_ref, kc_ref, vct_ref, gate_ref, ovt_ref,
                     o_ref, bias_ref, *, n_cmp):
    i = pl.program_id(2)
    tq = ATT_TILE
    nq = GQA_REP * tq
    n_slc = ovt_ref.shape[0]
    scale = HEAD_DIM ** -0.5

    def heads(x):
        return jnp.concatenate([x] * GQA_REP, axis=1)

    q = q_ref[...].astype(F32)
    qt = jnp.concatenate([q[:, r * HEAD_DIM:(r + 1) * HEAD_DIM].T for r in range(GQA_REP)],
                         axis=1).astype(BF16)
    row = lax.broadcasted_iota(jnp.int32, (tq, nq), 0)
    t_loc = lax.broadcasted_iota(jnp.int32, (tq, nq), 1) & (tq - 1)
    t_abs = i * tq + t_loc
    rel = t_loc - row
    causal = rel >= 0

    sc = _dot(kc_ref[...], qt) * scale
    ok_c = (row * CMP_STRIDE + (CMP_BLOCK - 1) <= t_abs) & (row < n_cmp)
    sc = jnp.where(ok_c, sc, NEG)
    e = jnp.exp(sc - jnp.max(sc, axis=0, keepdims=True))
    p_c = e / jnp.sum(e, axis=0, keepdims=True)
    p_c = jnp.where(t_abs >= CMP_BLOCK - 1, p_c, 0.0)
    o_c = _dot(vct_ref[...], p_c.astype(BF16))

    p_sum = p_c[:, 0:tq]
    for r in range(1, GQA_REP):
        p_sum = p_sum + p_c[:, r * tq:(r + 1) * tq]
    p_hi = p_sum.astype(BF16)
    p_lo = (p_sum - p_hi.astype(F32)).astype(BF16)
    ovt = ovt_ref[...]
    imp = _dot(ovt, p_hi) + _dot(ovt, p_lo)
    blk = lax.broadcasted_iota(jnp.int32, (n_slc, tq), 0)
    cur = (i * tq + lax.broadcasted_iota(jnp.int32, (n_slc, tq), 1)) >> int(math.log2(SLC_BLOCK))
    forced = (blk == 0) | (blk == cur) | (blk == cur - 1)
    imp = jnp.where(forced, BIG, imp)
    imp = jnp.where(blk <= cur, imp, NEG)
    rank = jnp.zeros((n_slc, tq), jnp.int32)
    for k in range(n_slc):
        other = imp[k:k + 1, :]
        ahead = (other > imp) | ((other == imp) & (blk > k))
        rank = rank + ahead.astype(jnp.int32)
    bias_ref[...] = jnp.where((rank < SLC_TOPK) & (blk <= cur), 0.0, NEG)

    init = (jnp.full((1, nq), NEG, F32), jnp.zeros((1, nq), F32), jnp.zeros((HEAD_DIM, nq), F32))
    half = tq // 2
    assert half == SLC_BLOCK

    def slc_scores(kt, n_tiles):
        start = pl.multiple_of(kt * tq, tq)
        s = _dot(ks_ref[pl.ds(start, n_tiles * tq), :], qt) * scale
        rows = [jnp.broadcast_to(bias_ref[pl.ds(2 * kt + j, 1), :], (half, tq))
                for j in range(2 * n_tiles)]
        return s + heads(jnp.concatenate(rows, axis=0))

    def win_scores(kt):
        start = pl.multiple_of(kt * tq, tq)
        return _dot(kw_ref[pl.ds(start, tq), :], qt) * scale

    def far_body(pair, c):
        kt = 2 * pair
        vt = jnp.concatenate([vst_ref[kt], vst_ref[kt + 1]], axis=1)
        return _attn_tile(slc_scores(kt, 2), vt, c)

    def near_body(kt, c):
        z_w = jnp.where(rel < WIN - (i - kt) * tq, win_scores(kt), NEG)
        return _attn_tile_pair(slc_scores(kt, 1), z_w, vst_ref[kt], vwt_ref[kt], c)

    n_far = jnp.maximum(i - WIN // tq, 0) // 2
    c_s = lax.fori_loop(0, n_far, far_body, init)
    both = tuple(jnp.concatenate([s, w], axis=1) for s, w in zip(c_s, init))
    both = lax.fori_loop(2 * n_far, i, near_body, both)
    _, l_sw, a_sw = _attn_tile_pair(jnp.where(causal, slc_scores(i, 1), NEG),
                                    jnp.where(causal, win_scores(i), NEG), vst_ref[i], vwt_ref[i], both)
    o_sw = a_sw / l_sw
    o_s = o_sw[:, :nq]
    o_w = o_sw[:, nq:]

    gt = gate_ref[...].T

    def gate(k):
        return jnp.concatenate([gt[r * N_GATES + k:r * N_GATES + k + 1, :] for r in range(GQA_REP)],
                               axis=1)

    o_t = gate(0) * o_c + gate(1) * o_s + gate(2) * o_w
    for r in range(GQA_REP):
        o_ref[:, r * HEAD_DIM:(r + 1) * HEAD_DIM] = o_t[:, r * tq:(r + 1) * tq].T.astype(o_ref.dtype)


def _nsa_attention(qk, vt, cmp, cvg, ovt, batch, seq):
    n = qk.shape[0]
    tq = ATT_TILE
    qt = seq // tq
    gw = GQA_REP * HEAD_DIM
    k0 = NSA_Q_DIM // HEAD_DIM
    gate0 = 2 * N_KV_HEADS
    n_win = seq // CMP_STRIDE

    def k_spec(which):
        return pl.BlockSpec((seq, HEAD_DIM), lambda b, g, i: (b, k0 + which * N_KV_HEADS + g))

    def vt_spec(which):
        return pl.BlockSpec((None, qt, None, HEAD_DIM, tq), lambda b, g, i: (which, b, g, 0, 0))

    def cmp_spec(which):
        return pl.BlockSpec((None, None, None, n_win, HEAD_DIM), lambda b, g, i: (b, which, g, 0, 0))

    return pl.pallas_call(
        functools.partial(_nsa_attn_kernel, n_cmp=n_win - CMP_BLOCK // CMP_STRIDE + 1),
        grid=(batch, N_KV_HEADS, qt),
        in_specs=[pl.BlockSpec((tq, gw), lambda b, g, i: (b * qt + i, g)),
                  k_spec(0), k_spec(1), vt_spec(0), vt_spec(1),
                  cmp_spec(0), cmp_spec(1),
                  pl.BlockSpec((tq, LANES), lambda b, g, i: (b * qt + i, gate0 + g)),
                  pl.BlockSpec(ovt.shape, lambda b, g, i: (0, 0))],
        out_specs=pl.BlockSpec((tq, gw), lambda b, g, i: (b * qt + i, g)),
        out_shape=jax.ShapeDtypeStruct((n, NSA_Q_DIM), BF16),
        scratch_shapes=[pltpu.VMEM(ovt.shape[:1] + (tq,), F32)],
        compiler_params=_cparams("parallel", "parallel", "arbitrary"),
        name="nsa_attention",
    )(qk, qk, qk, vt, vt, cmp, cmp, cvg, ovt)


def _rope_tables(seq):
    half = HEAD_DIM // 2
    inv = 1.0 / (ROPE_THETA ** (jnp.arange(half, dtype=F32) / half))
    ang = jnp.arange(seq, dtype=F32)[:, None] * inv[None, :]
    cos = jnp.cos(ang)
    sin = jnp.sin(ang)
    return jnp.concatenate([cos, cos], axis=1), jnp.concatenate([-sin, sin], axis=1)


def _overlap_matrix(seq):
    n_win = seq // CMP_STRIDE
    n_cmp = n_win - CMP_BLOCK // CMP_STRIDE + 1
    sj = np.arange(seq // SLC_BLOCK)[:, None]
    ci = np.arange(n_win)[None, :]
    ov = ((ci * CMP_STRIDE <= (sj + 1) * SLC_BLOCK - 1)
          & (ci * CMP_STRIDE + CMP_BLOCK - 1 >= sj * SLC_BLOCK) & (ci < n_cmp))
    return jnp.asarray(ov, BF16)


def _nsa_mixer(u, h, w_in, gate_b, cmp_pe, cmp_w1, cmp_b1, cmp_w2, cmp_b2, w_out, ga, gb, batch, seq):
    assert seq // CMP_STRIDE == ATT_TILE and seq % ROW_TILE == 0
    q0 = NSA_Q_DIM
    kvd = NSA_KV_DIM

    def part(k):
        return w_in[:, q0 + k * kvd:q0 + (k + 1) * kvd]

    w_qk = jnp.concatenate([w_in[:, :q0], part(2), part(4)], axis=1).astype(BF16)
    w_v = jnp.concatenate([part(3), part(5)], axis=1).astype(BF16)
    per_group = GQA_REP * N_GATES
    w_g = w_in[:, q0 + 6 * kvd:].reshape(D_MODEL, N_KV_HEADS, per_group)
    w_g = jnp.pad(w_g, ((0, 0), (0, 0), (0, LANES - per_group))).reshape(D_MODEL, N_KV_HEADS * LANES)
    b_g = jnp.pad(gate_b.reshape(N_KV_HEADS, per_group), ((0, 0), (0, LANES - per_group)))
    w_c = jnp.concatenate([part(0), part(1), w_g], axis=1).astype(BF16)
    bias_c = jnp.concatenate([jnp.zeros((2 * kvd,), F32), b_g.reshape(-1)])[None, :]
    cos, sin = _rope_tables(seq)

    tn = kvd
    qk = _nsa_proj(u, w_qk, jnp.zeros((1, w_qk.shape[1]), F32), cos, sin, tn=tn, out_dtype=BF16,
                   rope_chunks=tuple(range(w_qk.shape[1] // tn)), gate_chunks=(), seq=seq)
    vt = _nsa_vt(u, w_v)
    cvg = _nsa_proj(u, w_c, bias_c, cos, sin, tn=tn, out_dtype=F32,
                    rope_chunks=(0,), gate_chunks=(2,), seq=seq)
    cmp = _compress(cvg, cmp_pe, cmp_w1, cmp_b1[:, None, :], cmp_w2, cmp_b2[:, None, :], batch, seq)
    o = _nsa_attention(qk, vt, cmp, cvg, _overlap_matrix(seq), batch, seq)
    return _out_proj(o, w_out.astype(BF16), h, ga, gb)


def kernel(x, norm_g, ffn_w_up, ffn_conv_w, ffn_conv_b, ffn_w_down, pool_w_in, pool_w_grp, pool_scale,
           pool_w_out, sgu_w_in, sgu_ln_g, sgu_ln_b, sgu_w_s, sgu_b_s, sgu_w_out, nsa_w_in, nsa_gate_b,
           nsa_cmp_pe, nsa_cmp_w1, nsa_cmp_b1, nsa_cmp_w2, nsa_cmp_b2, nsa_w_out):
    batch, seq, d = x.shape
    h = x.reshape(batch * seq, d)

    def gain(i, k):
        return norm_g[i, k][None, :]

    u = _first_norm(h, gain(0, 0))
    for i in range(DEPTH):
        kind, j = i % 3, i // 3
        ga, gb = gain(i, 1), gain(i, 2)
        if kind == 0:
            h, u = _pool_mixer(u, h, pool_w_in[j].astype(BF16), pool_w_grp[j].astype(BF16),
                               pool_scale[j][None, :], pool_w_out[j].astype(BF16), ga, gb, seq)
        elif kind == 1:
            h, u = _sgu_mixer(u, h, sgu_w_in[j].astype(BF16), sgu_ln_g[j][None, :], sgu_ln_b[j][None, :],
                              sgu_w_s[j], sgu_b_s[j].T, sgu_w_out[j].astype(BF16), ga, gb)
        else:
            h, u = _nsa_mixer(u, h, nsa_w_in[j], nsa_gate_b[j], nsa_cmp_pe[j], nsa_cmp_w1[j],
                              nsa_cmp_b1[j], nsa_cmp_w2[j], nsa_cmp_b2[j], nsa_w_out[j], ga, gb,
                              batch, seq)
        g_next = gain(i + 1, 0) if i + 1 < DEPTH else gain(i, 3)
        h, u = _conv_ffn(u, h, ffn_w_up[i].astype(BF16), ffn_conv_w[i], ffn_conv_b[i][None, :],
                         ffn_w_down[i].astype(BF16), gain(i, 3), g_next, seq)
    return h.reshape(batch, seq, d)
```

```python
import functools
import math

import jax
import jax.numpy as jnp
import numpy as np
from jax import lax
from jax.experimental import pallas as pl
from jax.experimental.pallas import tpu as pltpu

F32 = jnp.float32
BF16 = jnp.bfloat16

D_MODEL = 2048
DEPTH = 4
RMS_EPS = 1e-6
LN_EPS = 1e-5
NEG = -1e30
BIG = 1e30

FFN_DIM = 5632
CONV_WIDTH = 3
POOL_WINDOWS = (2, 4, 8, 16)
POOL_GROUP_DIM = D_MODEL // len(POOL_WINDOWS)
SGU_CHUNK = 128
SGU_GROUPS = 16
SGU_GROUP_DIM = D_MODEL // SGU_GROUPS

HEAD_DIM = 128
N_HEADS = 16
N_KV_HEADS = 4
GQA_REP = N_HEADS // N_KV_HEADS
ROPE_THETA = 10000.0
CMP_BLOCK = 32
CMP_STRIDE = 16
CMP_HIDDEN = 2 * HEAD_DIM
SLC_BLOCK = 64
SLC_TOPK = 16
WIN = 512
NSA_Q_DIM = N_HEADS * HEAD_DIM
NSA_KV_DIM = N_KV_HEADS * HEAD_DIM
N_GATES = 3

LANES = 128
SUBLANES = 8
BF16_SUBLANES = 16
VMEM_LIMIT = 56 * 1024 * 1024

ROW_TILE = 512
PROJ_ROW_TILE = 1024
MIX_ROW_TILE = 256
FFN_ROW_TILE = 1024
FFN_COL_TILE = 512
ATT_TILE = 128


def _cparams(*sem):
    return pltpu.CompilerParams(dimension_semantics=sem, vmem_limit_bytes=VMEM_LIMIT)


def _const_spec(shape):
    nd = len(shape)
    return pl.BlockSpec(shape, lambda *_: (0,) * nd, pipeline_mode=pl.Buffered(1))


def _layer_spec(stacked_shape, layer):
    nd = len(stacked_shape) - 1
    return pl.BlockSpec((None,) + tuple(stacked_shape[1:]), lambda *_: (layer,) + (0,) * nd,
                        pipeline_mode=pl.Buffered(1))


def _rms(x):
    return x * lax.rsqrt(jnp.mean(x * x, axis=-1, keepdims=True) + RMS_EPS)


def _residual_update(m, h_ref, ga_ref, gb_ref, ho_ref, uo_ref):
    hn = h_ref[...] + _rms(m) * ga_ref[...]
    ho_ref[...] = hn
    uo_ref[...] = (_rms(hn) * gb_ref[...]).astype(BF16)


def _dot(a, b):
    return jnp.dot(a, b, preferred_element_type=F32)


def _halo_index(tm):
    blocks = tm // BF16_SUBLANES
    return lambda i, *_: (jnp.maximum(i * blocks - 1, 0), 0)


def _norm_kernel(h_ref, g_ref, u_ref):
    u_ref[...] = (_rms(h_ref[...]) * g_ref[...]).astype(BF16)


def _first_norm(h, g):
    n = h.shape[0]
    return pl.pallas_call(
        _norm_kernel,
        grid=(n // ROW_TILE,),
        in_specs=[pl.BlockSpec((ROW_TILE, D_MODEL), lambda i: (i, 0)),
                  pl.BlockSpec((1, D_MODEL), lambda i: (0, 0))],
        out_specs=pl.BlockSpec((ROW_TILE, D_MODEL), lambda i: (i, 0)),
        out_shape=jax.ShapeDtypeStruct((n, D_MODEL), BF16),
        compiler_params=_cparams("parallel"),
        name="first_norm",
    )(h, g)


def _out_proj_kernel(x_ref, w_ref, h_ref, ga_ref, gb_ref, ho_ref, uo_ref):
    _residual_update(_dot(x_ref[...], w_ref[...]), h_ref, ga_ref, gb_ref, ho_ref, uo_ref)


def _out_proj(xin, w, h, ga, gb):
    n, k = xin.shape
    tm = ROW_TILE
    row = lambda i: (i, 0)
    return pl.pallas_call(
        _out_proj_kernel,
        grid=(n // tm,),
        in_specs=[pl.BlockSpec((tm, k), row), _const_spec((k, D_MODEL)),
                  pl.BlockSpec((tm, D_MODEL), row),
                  pl.BlockSpec((1, D_MODEL), lambda i: (0, 0)),
                  pl.BlockSpec((1, D_MODEL), lambda i: (0, 0))],
        out_specs=[pl.BlockSpec((tm, D_MODEL), row), pl.BlockSpec((tm, D_MODEL), row)],
        out_shape=[jax.ShapeDtypeStruct((n, D_MODEL), F32),
                   jax.ShapeDtypeStruct((n, D_MODEL), BF16)],
        compiler_params=_cparams("parallel"),
        name="out_proj",
    )(xin, w, h, ga, gb)


def _ffn_kernel(u_ref, wa_ref, wb_ref, cw_ref, cb_ref, wd_ref, h_ref, ga_ref, gb_ref,
                ho_ref, uo_ref, tail_ref, *, seq_tiles):
    i = pl.program_id(0)
    c = pl.program_id(1)
    tm = u_ref.shape[0]
    keep = tail_ref.shape[1]

    @pl.when(i % seq_tiles == 0)
    def _():
        tail_ref[c] = jnp.zeros(tail_ref.shape[1:], F32)

    @pl.when(c == 0)
    def _():
        ho_ref[...] = jnp.zeros_like(ho_ref)

    u = u_ref[...]
    a = _dot(u, wa_ref[...])
    b = _dot(u, wb_ref[...])
    prev = tail_ref[c]
    tail_ref[c] = a[tm - keep:tm]
    row = lax.broadcasted_iota(jnp.int32, a.shape, 0)
    last1 = prev[keep - 1:keep]
    last2 = prev[keep - 2:keep - 1]
    p1 = jnp.where(row == 0, last1, pltpu.roll(a, 1, 0))
    p2 = jnp.where(row == 0, last2, jnp.where(row == 1, last1, pltpu.roll(a, 2, 0)))
    cw = cw_ref[...]
    y = cw[0:1] * p2 + cw[1:2] * p1 + cw[2:3] * a + cb_ref[...]
    gated = y / (1.0 + jnp.exp(-y)) * b
    ho_ref[...] += _dot(gated.astype(BF16), wd_ref[...])

    @pl.when(c == pl.num_programs(1) - 1)
    def _():
        _residual_update(ho_ref[...], h_ref, ga_ref, gb_ref, ho_ref, uo_ref)


def _conv_ffn(u, h, w_up, conv_w, conv_b, w_down, norm_g, layer, next_gain, seq):
    n = u.shape[0]
    tm, tf = FFN_ROW_TILE, FFN_COL_TILE
    n_chunks = FFN_DIM // tf
    row = lambda i, c: (i, 0)

    def row_spec():
        return pl.BlockSpec((tm, D_MODEL), row, pipeline_mode=pl.Buffered(1))

    def gain_spec(lyr, slot):
        return pl.BlockSpec((None, None, 1, D_MODEL), lambda i, c: (lyr, slot, 0, 0))

    return pl.pallas_call(
        functools.partial(_ffn_kernel, seq_tiles=seq // tm),
        grid=(n // tm, n_chunks),
        in_specs=[row_spec(),
                  pl.BlockSpec((None, D_MODEL, tf), lambda i, c: (layer, 0, c)),
                  pl.BlockSpec((None, D_MODEL, tf), lambda i, c: (layer, 0, c + n_chunks)),
                  pl.BlockSpec((None, CONV_WIDTH, tf), lambda i, c: (layer, 0, c)),
                  pl.BlockSpec((None, 1, tf), lambda i, c: (layer, 0, c)),
                  pl.BlockSpec((None, tf, D_MODEL), lambda i, c: (layer, c, 0)),
                  row_spec(),
                  gain_spec(layer, 3),
                  gain_spec(*next_gain)],
        out_specs=[row_spec(), row_spec()],
        out_shape=[jax.ShapeDtypeStruct((n, D_MODEL), F32),
                   jax.ShapeDtypeStruct((n, D_MODEL), BF16)],
        scratch_shapes=[pltpu.VMEM((n_chunks, SUBLANES, tf), F32)],
        compiler_params=_cparams("arbitrary", "arbitrary"),
        name="conv_ffn",
    )(u, w_up, w_up, conv_w, conv_b, w_down, h, norm_g, norm_g)


def _pool_kernel(u_ref, uh_ref, win_ref, wgrp_ref, scale_ref, wout_ref, h_ref, ga_ref, gb_ref,
                 ho_ref, uo_ref, *, seq_tiles):
    i = pl.program_id(0)
    tm = u_ref.shape[0]
    halo = BF16_SUBLANES
    win = win_ref[...]
    z = _dot(u_ref[...], win)
    zh = jnp.where(i % seq_tiles == 0, 0.0, _dot(uh_ref[...], win))
    tpos = (i % seq_tiles) * tm + lax.broadcasted_iota(jnp.int32, (tm, 1), 0)
    parts = []
    for g, w in enumerate(POOL_WINDOWS):
        cols = slice(g * POOL_GROUP_DIM, (g + 1) * POOL_GROUP_DIM)
        x = jnp.concatenate([zh[:, cols], z[:, cols]], axis=0)
        s = x
        k = 1
        while k < w:
            s = s + pltpu.roll(s, k, 0)
            k *= 2
        cnt = jnp.minimum(tpos + 1, w).astype(F32)
        p = s[halo:] / cnt - x[halo:]
        mg = _dot(p.astype(BF16), wgrp_ref[g]) * scale_ref[:, cols]
        parts.append(mg.astype(BF16))
    m = jnp.concatenate(parts, axis=1)
    _residual_update(_dot(m, wout_ref[...]), h_ref, ga_ref, gb_ref, ho_ref, uo_ref)


def _pool_mixer(u, h, w_in, w_grp, scale, w_out, ga, gb, layer, seq):
    n = u.shape[0]
    tm = MIX_ROW_TILE
    row = lambda i: (i, 0)
    return pl.pallas_call(
        functools.partial(_pool_kernel, seq_tiles=seq // tm),
        grid=(n // tm,),
        in_specs=[pl.BlockSpec((tm, D_MODEL), row),
                  pl.BlockSpec((BF16_SUBLANES, D_MODEL), _halo_index(tm)),
                  _layer_spec(w_in.shape, layer),
                  _layer_spec(w_grp.shape, layer),
                  _const_spec((1, D_MODEL)),
                  _layer_spec(w_out.shape, layer),
                  pl.BlockSpec((tm, D_MODEL), row),
                  _const_spec((1, D_MODEL)),
                  _const_spec((1, D_MODEL))],
        out_specs=[pl.BlockSpec((tm, D_MODEL), row), pl.BlockSpec((tm, D_MODEL), row)],
        out_shape=[jax.ShapeDtypeStruct((n, D_MODEL), F32),
                   jax.ShapeDtypeStruct((n, D_MODEL), BF16)],
        compiler_params=_cparams("parallel"),
        name="pool_mixer",
    )(u, u, w_in, w_grp, scale, w_out, h, ga, gb)


def _gelu_tanh(x):
    c = math.sqrt(2.0 / math.pi)
    return 0.5 * x * (1.0 + jnp.tanh(c * (x + 0.044715 * (x * x * x))))


def _sgu_kernel(u_ref, win_ref, lng_ref, lnb_ref, ws_ref, bst_ref, wout_ref, h_ref, ga_ref, gb_ref,
                ho_ref, uo_ref, gated_ref):
    tm = u_ref.shape[0]
    y = _gelu_tanh(_dot(u_ref[...], win_ref[...]))
    uu = y[:, :D_MODEL]
    v = y[:, D_MODEL:]
    mu = jnp.mean(v, axis=-1, keepdims=True)
    vc = v - mu
    var = jnp.mean(vc * vc, axis=-1, keepdims=True)
    vn = (vc * lax.rsqrt(var + LN_EPS) * lng_ref[...] + lnb_ref[...]).astype(BF16)
    t_idx = lax.broadcasted_iota(jnp.int32, (SGU_CHUNK, SGU_CHUNK), 0)
    s_idx = lax.broadcasted_iota(jnp.int32, (SGU_CHUNK, SGU_CHUNK), 1)
    causal = s_idx <= t_idx
    bst = bst_ref[...]
    for g in range(SGU_GROUPS):
        cols = slice(g * SGU_GROUP_DIM, (g + 1) * SGU_GROUP_DIM)
        ws = jnp.where(causal, ws_ref[g], 0.0).astype(BF16)
        bias = bst[:, g:g + 1]
        for ci in range(tm // SGU_CHUNK):
            rows = slice(ci * SGU_CHUNK, (ci + 1) * SGU_CHUNK)
            mixed = _dot(ws, vn[rows, cols]) + bias
            gated_ref[rows, cols] = (uu[rows, cols] * mixed).astype(BF16)
    _residual_update(_dot(gated_ref[...], wout_ref[...]), h_ref, ga_ref, gb_ref, ho_ref, uo_ref)


def _sgu_mixer(u, h, w_in, ln_g, ln_b, w_s, b_s_t, w_out, ga, gb):
    n = u.shape[0]
    tm = MIX_ROW_TILE
    row = lambda i: (i, 0)
    return pl.pallas_call(
        _sgu_kernel,
        grid=(n // tm,),
        in_specs=[pl.BlockSpec((tm, D_MODEL), row),
                  _const_spec((D_MODEL, 2 * D_MODEL)),
                  _const_spec((1, D_MODEL)),
                  _const_spec((1, D_MODEL)),
                  _const_spec(w_s.shape),
                  _const_spec(b_s_t.shape),
                  _const_spec((D_MODEL, D_MODEL)),
                  pl.BlockSpec((tm, D_MODEL), row),
                  _const_spec((1, D_MODEL)),
                  _const_spec((1, D_MODEL))],
        out_specs=[pl.BlockSpec((tm, D_MODEL), row), pl.BlockSpec((tm, D_MODEL), row)],
        out_shape=[jax.ShapeDtypeStruct((n, D_MODEL), F32),
                   jax.ShapeDtypeStruct((n, D_MODEL), BF16)],
        scratch_shapes=[pltpu.VMEM((tm, D_MODEL), BF16)],
        compiler_params=_cparams("parallel"),
        name="sgu_mixer",
    )(u, w_in, ln_g, ln_b, w_s, b_s_t, w_out, h, ga, gb)


def _nsa_proj_kernel(u_ref, w_ref, bias_ref, cos_ref, sin_ref, o_ref, *, rope_chunks, gate_chunks):
    j = pl.program_id(1)
    acc = _dot(u_ref[...], w_ref[...])
    tn = acc.shape[1]

    def is_in(chunks):
        hit = j == chunks[0]
        for c in chunks[1:]:
            hit = hit | (j == c)
        return hit

    rope = is_in(rope_chunks)
    plain = jnp.logical_not(rope)
    if gate_chunks:
        gate = is_in(gate_chunks)
        plain = plain & jnp.logical_not(gate)

        @pl.when(gate)
        def _():
            o_ref[...] = (1.0 / (1.0 + jnp.exp(-(acc + bias_ref[...])))).astype(o_ref.dtype)

    @pl.when(rope)
    def _():
        cos = cos_ref[...]
        sin = sin_ref[...]
        for hh in range(tn // HEAD_DIM):
            cols = slice(hh * HEAD_DIM, (hh + 1) * HEAD_DIM)
            x = acc[:, cols]
            o_ref[:, cols] = (x * cos + pltpu.roll(x, HEAD_DIM // 2, 1) * sin).astype(o_ref.dtype)

    @pl.when(plain)
    def _():
        o_ref[...] = acc.astype(o_ref.dtype)


def _nsa_proj(u, w, bias, cos, sin, *, tn, out_dtype, rope_chunks, gate_chunks, seq):
    n = u.shape[0]
    ncols = w.shape[1]
    tm = PROJ_ROW_TILE
    seq_tiles = seq // tm
    return pl.pallas_call(
        functools.partial(_nsa_proj_kernel, rope_chunks=rope_chunks, gate_chunks=gate_chunks),
        grid=(n // tm, ncols // tn),
        in_specs=[pl.BlockSpec((tm, D_MODEL), lambda i, j: (i, 0)),
                  pl.BlockSpec((D_MODEL, tn), lambda i, j: (0, j)),
                  pl.BlockSpec((1, tn), lambda i, j: (0, j)),
                  pl.BlockSpec((tm, HEAD_DIM), lambda i, j: (i % seq_tiles, 0)),
                  pl.BlockSpec((tm, HEAD_DIM), lambda i, j: (i % seq_tiles, 0))],
        out_specs=pl.BlockSpec((tm, tn), lambda i, j: (i, j)),
        out_shape=jax.ShapeDtypeStruct((n, ncols), out_dtype),
        compiler_params=_cparams("parallel", "arbitrary"),
        name="nsa_proj",
    )(u, w, bias, cos, sin)


def _nsa_vt_kernel(u_ref, w_ref, o_ref):
    acc = _dot(u_ref[...], w_ref[...])
    for kt in range(acc.shape[0] // ATT_TILE):
        for g in range(N_KV_HEADS):
            tile = acc[kt * ATT_TILE:(kt + 1) * ATT_TILE, g * HEAD_DIM:(g + 1) * HEAD_DIM]
            o_ref[kt, g] = tile.T.astype(o_ref.dtype)


def _nsa_vt(u, w):
    n = u.shape[0]
    tm = PROJ_ROW_TILE
    kt = tm // ATT_TILE
    return pl.pallas_call(
        _nsa_vt_kernel,
        grid=(n // tm, 2),
        in_specs=[pl.BlockSpec((tm, D_MODEL), lambda i, j: (i, 0)),
                  pl.BlockSpec((D_MODEL, NSA_KV_DIM), lambda i, j: (0, j))],
        out_specs=pl.BlockSpec((None, kt, N_KV_HEADS, HEAD_DIM, ATT_TILE), lambda i, j: (j, i, 0, 0, 0)),
        out_shape=jax.ShapeDtypeStruct((2, n // ATT_TILE, N_KV_HEADS, HEAD_DIM, ATT_TILE), BF16),
        compiler_params=_cparams("parallel", "arbitrary"),
        name="nsa_vt",
    )(u, w)


def _compress_kernel(a_ref, pe_ref, w1_ref, b1_ref, w2_ref, b2_ref, o_ref):
    nb = a_ref.shape[0] // CMP_STRIDE
    top = jnp.zeros((nb, CMP_HIDDEN), F32)
    bot = jnp.zeros((nb, CMP_HIDDEN), F32)
    for p in range(CMP_STRIDE):
        xp = a_ref[pl.ds(p, nb, stride=CMP_STRIDE), :]
        lo = slice(p * HEAD_DIM, (p + 1) * HEAD_DIM)
        hi = slice((CMP_STRIDE + p) * HEAD_DIM, (CMP_STRIDE + p + 1) * HEAD_DIM)
        top += _dot((xp + pe_ref[p:p + 1, :]).astype(BF16), w1_ref[lo, :].astype(BF16))
        bot += _dot((xp + pe_ref[CMP_STRIDE + p:CMP_STRIDE + p + 1, :]).astype(BF16),
                    w1_ref[hi, :].astype(BF16))
    hid = _gelu_tanh(top + pltpu.roll(bot, nb - 1, 0) + b1_ref[...])
    res = _dot(hid.astype(BF16), w2_ref[...].astype(BF16)) + b2_ref[...]

    @pl.when(pl.program_id(1) == 0)
    def _():
        o_ref[...] = res.astype(o_ref.dtype)

    @pl.when(pl.program_id(1) == 1)
    def _():
        o_ref[...] = res.T.astype(o_ref.dtype)


def _compress(cv, pe, w1, b1, w2, b2, batch, seq):
    nb = seq // CMP_STRIDE
    return pl.pallas_call(
        _compress_kernel,
        grid=(batch, 2, N_KV_HEADS),
        in_specs=[pl.BlockSpec((seq, HEAD_DIM), lambda b, s, g: (b, s * N_KV_HEADS + g)),
                  pl.BlockSpec((None, CMP_BLOCK, HEAD_DIM), lambda b, s, g: (s, 0, 0)),
                  pl.BlockSpec((None, CMP_BLOCK * HEAD_DIM, CMP_HIDDEN), lambda b, s, g: (s, 0, 0)),
                  pl.BlockSpec((None, 1, CMP_HIDDEN), lambda b, s, g: (s, 0, 0)),
                  pl.BlockSpec((None, CMP_HIDDEN, HEAD_DIM), lambda b, s, g: (s, 0, 0)),
                  pl.BlockSpec((None, 1, HEAD_DIM), lambda b, s, g: (s, 0, 0))],
        out_specs=pl.BlockSpec((None, None, None, nb, HEAD_DIM), lambda b, s, g: (b, s, g, 0, 0)),
        out_shape=jax.ShapeDtypeStruct((batch, 2, N_KV_HEADS, nb, HEAD_DIM), BF16),
        compiler_params=_cparams("parallel", "parallel", "parallel"),
        name="nsa_compress",
    )(cv, pe, w1, b1, w2, b2)


def _attn_tile(z, vt, carry):
    m, l, acc = carry
    m_new = jnp.maximum(m, jnp.max(z, axis=0, keepdims=True))
    alpha = jnp.exp(m - m_new)
    p = jnp.exp(z - m_new)
    l_new = alpha * l + jnp.sum(p, axis=0, keepdims=True)
    acc_new = alpha * acc + _dot(vt, p.astype(BF16))
    return m_new, l_new, acc_new


def _attn_tile_pair(z_a, z_b, vt_a, vt_b, carry):
    m, l, acc = carry
    nq = z_a.shape[1]
    z = jnp.concatenate([z_a, z_b], axis=1)
    m_new = jnp.maximum(m, jnp.max(z, axis=0, keepdims=True))
    alpha = jnp.exp(m - m_new)
    p = jnp.exp(z - m_new)
    l_new = alpha * l + jnp.sum(p, axis=0, keepdims=True)
    p = p.astype(BF16)
    pv = jnp.concatenate([_dot(vt_a, p[:, :nq]), _dot(vt_b, p[:, nq:])], axis=1)
    return m_new, l_new, alpha * acc + pv


def _nsa_attn_kernel(q_ref, ks_ref, kw_ref, vst_ref, vwt_ref, kc_ref, vct_ref, gate_ref, ovt_ref,
                     o_ref, bias_ref, *, n_cmp):
    i = pl.program_id(2)
    tq = ATT_TILE
    nq = GQA_REP * tq
    n_slc = ovt_ref.shape[0]
    scale = HEAD_DIM ** -0.5

    def heads(x):
        return jnp.concatenate([x] * GQA_REP, axis=1)

    q = q_ref[...].astype(F32)
    qt = jnp.concatenate([q[:, r * HEAD_DIM:(r + 1) * HEAD_DIM].T for r in range(GQA_REP)],
                         axis=1).astype(BF16)
    row = lax.broadcasted_iota(jnp.int32, (tq, nq), 0)
    t_loc = lax.broadcasted_iota(jnp.int32, (tq, nq), 1) & (tq - 1)
    t_abs = i * tq + t_loc
    rel = t_loc - row
    causal = rel >= 0

    sc = _dot(kc_ref[...], qt) * scale
    ok_c = (row * CMP_STRIDE + (CMP_BLOCK - 1) <= t_abs) & (row < n_cmp)
    sc = jnp.where(ok_c, sc, NEG)
    e = jnp.exp(sc - jnp.max(sc, axis=0, keepdims=True))
    p_c = e / jnp.sum(e, axis=0, keepdims=True)
    p_c = jnp.where(t_abs >= CMP_BLOCK - 1, p_c, 0.0)
    o_c = _dot(vct_ref[...], p_c.astype(BF16))

    p_sum = p_c[:, 0:tq]
    for r in range(1, GQA_REP):
        p_sum = p_sum + p_c[:, r * tq:(r + 1) * tq]
    p_hi = p_sum.astype(BF16)
    p_lo = (p_sum - p_hi.astype(F32)).astype(BF16)
    ovt = ovt_ref[...]
    imp = _dot(ovt, p_hi) + _dot(ovt, p_lo)
    blk = lax.broadcasted_iota(jnp.int32, (n_slc, tq), 0)
    cur = (i * tq + lax.broadcasted_iota(jnp.int32, (n_slc, tq), 1)) >> int(math.log2(SLC_BLOCK))
    forced = (blk == 0) | (blk == cur) | (blk == cur - 1)
    imp = jnp.where(forced, BIG, imp)
    imp = jnp.where(blk <= cur, imp, NEG)
    rank = jnp.zeros((n_slc, tq), jnp.int32)
    for k in range(n_slc):
        other = imp[k:k + 1, :]
        ahead = (other > imp) | ((other == imp) & (blk > k))
        rank = rank + ahead.astype(jnp.int32)
    bias_ref[...] = jnp.where((rank < SLC_TOPK) & (blk <= cur), 0.0, NEG)

    init = (jnp.full((1, nq), NEG, F32), jnp.zeros((1, nq), F32), jnp.zeros((HEAD_DIM, nq), F32))
    half = tq // 2
    assert half == SLC_BLOCK

    def slc_scores(kt, n_tiles):
        start = pl.multiple_of(kt * tq, tq)
        s = _dot(ks_ref[pl.ds(start, n_tiles * tq), :], qt) * scale
        rows = [jnp.broadcast_to(bias_ref[pl.ds(2 * kt + j, 1), :], (half, tq))
                for j in range(2 * n_tiles)]
        return s + heads(jnp.concatenate(rows, axis=0))

    def win_scores(kt):
        start = pl.multiple_of(kt * tq, tq)
        return _dot(kw_ref[pl.ds(start, tq), :], qt) * scale

    def far_body(pair, c):
        kt = 2 * pair
        vt = jnp.concatenate([vst_ref[kt], vst_ref[kt + 1]], axis=1)
        return _attn_tile(slc_scores(kt, 2), vt, c)

    def near_body(kt, c):
        z_w = jnp.where(rel < WIN - (i - kt) * tq, win_scores(kt), NEG)
        return _attn_tile_pair(slc_scores(kt, 1), z_w, vst_ref[kt], vwt_ref[kt], c)

    n_far = jnp.maximum(i - WIN // tq, 0) // 2
    c_s = lax.fori_loop(0, n_far, far_body, init)
    both = tuple(jnp.concatenate([s, w], axis=1) for s, w in zip(c_s, init))
    both = lax.fori_loop(2 * n_far, i, near_body, both)
    _, l_sw, a_sw = _attn_tile_pair(jnp.where(causal, slc_scores(i, 1), NEG),
                                    jnp.where(causal, win_scores(i), NEG), vst_ref[i], vwt_ref[i], both)
    o_sw = a_sw / l_sw
    o_s = o_sw[:, :nq]
    o_w = o_sw[:, nq:]

    gt = gate_ref[...].T

    def gate(k):
        return jnp.concatenate([gt[r * N_GATES + k:r * N_GATES + k + 1, :] for r in range(GQA_REP)],
                               axis=1)

    o_t = gate(0) * o_c + gate(1) * o_s + gate(2) * o_w
    for r in range(GQA_REP):
        o_ref[:, r * HEAD_DIM:(r + 1) * HEAD_DIM] = o_t[:, r * tq:(r + 1) * tq].T.astype(o_ref.dtype)


def _nsa_attention(qk, vt, cmp, cvg, ovt, batch, seq):
    n = qk.shape[0]
    tq = ATT_TILE
    qt = seq // tq
    gw = GQA_REP * HEAD_DIM
    k0 = NSA_Q_DIM // HEAD_DIM
    gate0 = 2 * N_KV_HEADS
    n_win = seq // CMP_STRIDE

    def k_spec(which):
        return pl.BlockSpec((seq, HEAD_DIM), lambda b, g, i: (b, k0 + which * N_KV_HEADS + g))

    def vt_spec(which):
        return pl.BlockSpec((None, qt, None, HEAD_DIM, tq), lambda b, g, i: (which, b, g, 0, 0))

    def cmp_spec(which):
        return pl.BlockSpec((None, None, None, n_win, HEAD_DIM), lambda b, g, i: (b, which, g, 0, 0))

    return pl.pallas_call(
        functools.partial(_nsa_attn_kernel, n_cmp=n_win - CMP_BLOCK // CMP_STRIDE + 1),
        grid=(batch, N_KV_HEADS, qt),
        in_specs=[pl.BlockSpec((tq, gw), lambda b, g, i: (b * qt + i, g)),
                  k_spec(0), k_spec(1), vt_spec(0), vt_spec(1),
                  cmp_spec(0), cmp_spec(1),
                  pl.BlockSpec((tq, LANES), lambda b, g, i: (b * qt + i, gate0 + g)),
                  pl.BlockSpec(ovt.shape, lambda b, g, i: (0, 0))],
        out_specs=pl.BlockSpec((tq, gw), lambda b, g, i: (b * qt + i, g)),
        out_shape=jax.ShapeDtypeStruct((n, NSA_Q_DIM), BF16),
        scratch_shapes=[pltpu.VMEM(ovt.shape[:1] + (tq,), F32)],
        compiler_params=_cparams("parallel", "parallel", "arbitrary"),
        name="nsa_attention",
    )(qk, qk, qk, vt, vt, cmp, cmp, cvg, ovt)


def _rope_tables(seq):
    half = HEAD_DIM // 2
    inv = 1.0 / (ROPE_THETA ** (jnp.arange(half, dtype=F32) / half))
    ang = jnp.arange(seq, dtype=F32)[:, None] * inv[None, :]
    cos = jnp.cos(ang)
    sin = jnp.sin(ang)
    return jnp.concatenate([cos, cos], axis=1), jnp.concatenate([-sin, sin], axis=1)


def _overlap_matrix(seq):
    n_win = seq // CMP_STRIDE
    n_cmp = n_win - CMP_BLOCK // CMP_STRIDE + 1
    sj = np.arange(seq // SLC_BLOCK)[:, None]
    ci = np.arange(n_win)[None, :]
    ov = ((ci * CMP_STRIDE <= (sj + 1) * SLC_BLOCK - 1)
          & (ci * CMP_STRIDE + CMP_BLOCK - 1 >= sj * SLC_BLOCK) & (ci < n_cmp))
    return jnp.asarray(ov, BF16)


def _nsa_mixer(u, h, w_in, gate_b, cmp_pe, cmp_w1, cmp_b1, cmp_w2, cmp_b2, w_out, ga, gb, batch, seq):
    assert seq // CMP_STRIDE == ATT_TILE and seq % PROJ_ROW_TILE == 0
    q0 = NSA_Q_DIM
    kvd = NSA_KV_DIM

    def part(k):
        return w_in[:, q0 + k * kvd:q0 + (k + 1) * kvd]

    w_qk = jnp.concatenate([w_in[:, :q0], part(2), part(4)], axis=1).astype(BF16)
    w_v = jnp.concatenate([part(3), part(5)], axis=1).astype(BF16)
    per_group = GQA_REP * N_GATES
    w_g = w_in[:, q0 + 6 * kvd:].reshape(D_MODEL, N_KV_HEADS, per_group)
    w_g = jnp.pad(w_g, ((0, 0), (0, 0), (0, LANES - per_group))).reshape(D_MODEL, N_KV_HEADS * LANES)
    b_g = jnp.pad(gate_b.reshape(N_KV_HEADS, per_group), ((0, 0), (0, LANES - per_group)))
    w_c = jnp.concatenate([part(0), part(1), w_g], axis=1).astype(BF16)
    bias_c = jnp.concatenate([jnp.zeros((2 * kvd,), F32), b_g.reshape(-1)])[None, :]
    cos, sin = _rope_tables(seq)

    tn = kvd
    qk = _nsa_proj(u, w_qk, jnp.zeros((1, w_qk.shape[1]), F32), cos, sin, tn=tn, out_dtype=BF16,
                   rope_chunks=tuple(range(w_qk.shape[1] // tn)), gate_chunks=(), seq=seq)
    vt = _nsa_vt(u, w_v)
    cvg = _nsa_proj(u, w_c, bias_c, cos, sin, tn=tn, out_dtype=F32,
                    rope_chunks=(0,), gate_chunks=(2,), seq=seq)
    cmp = _compress(cvg, cmp_pe, cmp_w1, cmp_b1[:, None, :], cmp_w2, cmp_b2[:, None, :], batch, seq)
    o = _nsa_attention(qk, vt, cmp, cvg, _overlap_matrix(seq), batch, seq)
    return _out_proj(o, w_out.astype(BF16), h, ga, gb)


def kernel(x, norm_g, ffn_w_up, ffn_conv_w, ffn_conv_b, ffn_w_down, pool_w_in, pool_w_grp, pool_scale,
           pool_w_out, sgu_w_in, sgu_ln_g, sgu_ln_b, sgu_w_s, sgu_b_s, sgu_w_out, nsa_w_in, nsa_gate_b,
           nsa_cmp_pe, nsa_cmp_w1, nsa_cmp_b1, nsa_cmp_w2, nsa_cmp_b2, nsa_w_out):
    batch, seq, d = x.shape
    h = x.reshape(batch * seq, d)

    def gain(i, k):
        return norm_g[i, k][None, :]

    ffn_up, ffn_down = ffn_w_up.astype(BF16), ffn_w_down.astype(BF16)
    pool_in, pool_grp, pool_out = pool_w_in.astype(BF16), pool_w_grp.astype(BF16), pool_w_out.astype(BF16)
    gains = norm_g[:, :, None, :]
    conv_b = ffn_conv_b[:, None, :]

    u = _first_norm(h, gain(0, 0))
    for i in range(DEPTH):
        kind, j = i % 3, i // 3
        ga, gb = gain(i, 1), gain(i, 2)
        if kind == 0:
            h, u = _pool_mixer(u, h, pool_in, pool_grp, pool_scale[j][None, :], pool_out, ga, gb, j, seq)
        elif kind == 1:
            h, u = _sgu_mixer(u, h, sgu_w_in[j].astype(BF16), sgu_ln_g[j][None, :], sgu_ln_b[j][None, :],
                              sgu_w_s[j], sgu_b_s[j].T, sgu_w_out[j].astype(BF16), ga, gb)
        else:
            h, u = _nsa_mixer(u, h, nsa_w_in[j], nsa_gate_b[j], nsa_cmp_pe[j], nsa_cmp_w1[j],
                              nsa_cmp_b1[j], nsa_cmp_w2[j], nsa_cmp_b2[j], nsa_w_out[j], ga, gb,
                              batch, seq)
        next_gain = (i + 1, 0) if i + 1 < DEPTH else (i, 3)
        h, u = _conv_ffn(u, h, ffn_up, ffn_conv_w, conv_b, ffn_down, gains, i, next_gain, seq)
    return h.reshape(batch, seq, d)
```

```python
import functools
import math

import jax
import jax.numpy as jnp
import numpy as np
from jax import lax
from jax.experimental import pallas as pl
from jax.experimental.pallas import tpu as pltpu

F32 = jnp.float32
BF16 = jnp.bfloat16

D_MODEL = 2048
DEPTH = 4
RMS_EPS = 1e-6
LN_EPS = 1e-5
NEG = -1e30
BIG = 1e30

FFN_DIM = 5632
CONV_WIDTH = 3
POOL_WINDOWS = (2, 4, 8, 16)
POOL_GROUP_DIM = D_MODEL // len(POOL_WINDOWS)
SGU_CHUNK = 128
SGU_GROUPS = 16
SGU_GROUP_DIM = D_MODEL // SGU_GROUPS

HEAD_DIM = 128
N_HEADS = 16
N_KV_HEADS = 4
GQA_REP = N_HEADS // N_KV_HEADS
ROPE_THETA = 10000.0
CMP_BLOCK = 32
CMP_STRIDE = 16
CMP_HIDDEN = 2 * HEAD_DIM
SLC_BLOCK = 64
SLC_TOPK = 16
WIN = 512
NSA_Q_DIM = N_HEADS * HEAD_DIM
NSA_KV_DIM = N_KV_HEADS * HEAD_DIM
N_GATES = 3

LANES = 128
SUBLANES = 8
BF16_SUBLANES = 16
VMEM_LIMIT = 56 * 1024 * 1024

ROW_TILE = 512
PROJ_ROW_TILE = 1024
MIX_ROW_TILE = 256
FFN_ROW_TILE = 1024
FFN_COL_TILE = 512
ATT_TILE = 128
ATT_K_STEP = 256
ATT_Q_TILE = 256


def _cparams(*sem):
    return pltpu.CompilerParams(dimension_semantics=sem, vmem_limit_bytes=VMEM_LIMIT)


def _const_spec(shape):
    nd = len(shape)
    return pl.BlockSpec(shape, lambda *_: (0,) * nd, pipeline_mode=pl.Buffered(1))


def _layer_spec(stacked_shape, layer):
    nd = len(stacked_shape) - 1
    return pl.BlockSpec((None,) + tuple(stacked_shape[1:]), lambda *_: (layer,) + (0,) * nd,
                        pipeline_mode=pl.Buffered(1))


def _rms(x):
    return x * lax.rsqrt(jnp.mean(x * x, axis=-1, keepdims=True) + RMS_EPS)


def _residual_update(m, h_ref, ga_ref, gb_ref, ho_ref, uo_ref):
    hn = h_ref[...] + _rms(m) * ga_ref[...]
    ho_ref[...] = hn
    uo_ref[...] = (_rms(hn) * gb_ref[...]).astype(BF16)


def _dot(a, b):
    return jnp.dot(a, b, preferred_element_type=F32)


def _halo_index(tm):
    blocks = tm // BF16_SUBLANES
    return lambda i, *_: (jnp.maximum(i * blocks - 1, 0), 0)


def _norm_kernel(h_ref, g_ref, u_ref):
    u_ref[...] = (_rms(h_ref[...]) * g_ref[...]).astype(BF16)


def _first_norm(h, g):
    n = h.shape[0]
    return pl.pallas_call(
        _norm_kernel,
        grid=(n // ROW_TILE,),
        in_specs=[pl.BlockSpec((ROW_TILE, D_MODEL), lambda i: (i, 0)),
                  pl.BlockSpec((1, D_MODEL), lambda i: (0, 0))],
        out_specs=pl.BlockSpec((ROW_TILE, D_MODEL), lambda i: (i, 0)),
        out_shape=jax.ShapeDtypeStruct((n, D_MODEL), BF16),
        compiler_params=_cparams("parallel"),
        name="first_norm",
    )(h, g)


def _out_proj_kernel(x_ref, w_ref, h_ref, ga_ref, gb_ref, ho_ref, uo_ref):
    _residual_update(_dot(x_ref[...], w_ref[...]), h_ref, ga_ref, gb_ref, ho_ref, uo_ref)


def _out_proj(xin, w, h, ga, gb):
    n, k = xin.shape
    tm = ROW_TILE
    row = lambda i: (i, 0)
    return pl.pallas_call(
        _out_proj_kernel,
        grid=(n // tm,),
        in_specs=[pl.BlockSpec((tm, k), row), _const_spec((k, D_MODEL)),
                  pl.BlockSpec((tm, D_MODEL), row),
                  pl.BlockSpec((1, D_MODEL), lambda i: (0, 0)),
                  pl.BlockSpec((1, D_MODEL), lambda i: (0, 0))],
        out_specs=[pl.BlockSpec((tm, D_MODEL), row), pl.BlockSpec((tm, D_MODEL), row)],
        out_shape=[jax.ShapeDtypeStruct((n, D_MODEL), F32),
                   jax.ShapeDtypeStruct((n, D_MODEL), BF16)],
        compiler_params=_cparams("parallel"),
        name="out_proj",
    )(xin, w, h, ga, gb)


def _ffn_kernel(u_ref, wa_ref, wb_ref, cw_ref, cb_ref, wd_ref, h_ref, ga_ref, gb_ref,
                ho_ref, uo_ref, tail_ref, *, seq_tiles):
    i = pl.program_id(0)
    c = pl.program_id(1)
    tm = u_ref.shape[0]
    keep = tail_ref.shape[1]

    @pl.when(i % seq_tiles == 0)
    def _():
        tail_ref[c] = jnp.zeros(tail_ref.shape[1:], F32)

    @pl.when(c == 0)
    def _():
        ho_ref[...] = jnp.zeros_like(ho_ref)

    u = u_ref[...]
    a = _dot(u, wa_ref[...])
    b = _dot(u, wb_ref[...])
    prev = tail_ref[c]
    tail_ref[c] = a[tm - keep:tm]
    row = lax.broadcasted_iota(jnp.int32, a.shape, 0)
    last1 = prev[keep - 1:keep]
    last2 = prev[keep - 2:keep - 1]
    p1 = jnp.where(row == 0, last1, pltpu.roll(a, 1, 0))
    p2 = jnp.where(row == 0, last2, jnp.where(row == 1, last1, pltpu.roll(a, 2, 0)))
    cw = cw_ref[...]
    y = cw[0:1] * p2 + cw[1:2] * p1 + cw[2:3] * a + cb_ref[...]
    gated = y / (1.0 + jnp.exp(-y)) * b
    ho_ref[...] += _dot(gated.astype(BF16), wd_ref[...])

    @pl.when(c == pl.num_programs(1) - 1)
    def _():
        _residual_update(ho_ref[...], h_ref, ga_ref, gb_ref, ho_ref, uo_ref)


def _conv_ffn(u, h, w_up, conv_w, conv_b, w_down, norm_g, layer, next_gain, seq):
    n = u.shape[0]
    tm, tf = FFN_ROW_TILE, FFN_COL_TILE
    n_chunks = FFN_DIM // tf
    row = lambda i, c: (i, 0)

    def row_spec(buffers=1):
        return pl.BlockSpec((tm, D_MODEL), row, pipeline_mode=pl.Buffered(buffers))

    def gain_spec(lyr, slot):
        return pl.BlockSpec((None, None, 1, D_MODEL), lambda i, c: (lyr, slot, 0, 0))

    return pl.pallas_call(
        functools.partial(_ffn_kernel, seq_tiles=seq // tm),
        grid=(n // tm, n_chunks),
        in_specs=[row_spec(2),
                  pl.BlockSpec((None, D_MODEL, tf), lambda i, c: (layer, 0, c)),
                  pl.BlockSpec((None, D_MODEL, tf), lambda i, c: (layer, 0, c + n_chunks)),
                  pl.BlockSpec((None, CONV_WIDTH, tf), lambda i, c: (layer, 0, c)),
                  pl.BlockSpec((None, 1, tf), lambda i, c: (layer, 0, c)),
                  pl.BlockSpec((None, tf, D_MODEL), lambda i, c: (layer, c, 0)),
                  row_spec(),
                  gain_spec(layer, 3),
                  gain_spec(*next_gain)],
        out_specs=[row_spec(), row_spec(2)],
        out_shape=[jax.ShapeDtypeStruct((n, D_MODEL), F32),
                   jax.ShapeDtypeStruct((n, D_MODEL), BF16)],
        scratch_shapes=[pltpu.VMEM((n_chunks, SUBLANES, tf), F32)],
        compiler_params=_cparams("arbitrary", "arbitrary"),
        name="conv_ffn",
    )(u, w_up, w_up, conv_w, conv_b, w_down, h, norm_g, norm_g)


def _pool_kernel(u_ref, uh_ref, win_ref, wgrp_ref, scale_ref, wout_ref, h_ref, ga_ref, gb_ref,
                 ho_ref, uo_ref, *, seq_tiles):
    i = pl.program_id(0)
    tm = u_ref.shape[0]
    halo = BF16_SUBLANES
    win = win_ref[...]
    z = _dot(u_ref[...], win)
    zh = jnp.where(i % seq_tiles == 0, 0.0, _dot(uh_ref[...], win))
    tpos = (i % seq_tiles) * tm + lax.broadcasted_iota(jnp.int32, (tm, 1), 0)
    parts = []
    for g, w in enumerate(POOL_WINDOWS):
        cols = slice(g * POOL_GROUP_DIM, (g + 1) * POOL_GROUP_DIM)
        x = jnp.concatenate([zh[:, cols], z[:, cols]], axis=0)
        s = x
        k = 1
        while k < w:
            s = s + pltpu.roll(s, k, 0)
            k *= 2
        cnt = jnp.minimum(tpos + 1, w).astype(F32)
        p = s[halo:] / cnt - x[halo:]
        mg = _dot(p.astype(BF16), wgrp_ref[g]) * scale_ref[:, cols]
        parts.append(mg.astype(BF16))
    m = jnp.concatenate(parts, axis=1)
    _residual_update(_dot(m, wout_ref[...]), h_ref, ga_ref, gb_ref, ho_ref, uo_ref)


def _pool_mixer(u, h, w_in, w_grp, scale, w_out, ga, gb, layer, seq):
    n = u.shape[0]
    tm = MIX_ROW_TILE
    row = lambda i: (i, 0)
    return pl.pallas_call(
        functools.partial(_pool_kernel, seq_tiles=seq // tm),
        grid=(n // tm,),
        in_specs=[pl.BlockSpec((tm, D_MODEL), row),
                  pl.BlockSpec((BF16_SUBLANES, D_MODEL), _halo_index(tm)),
                  _layer_spec(w_in.shape, layer),
                  _layer_spec(w_grp.shape, layer),
                  _const_spec((1, D_MODEL)),
                  _layer_spec(w_out.shape, layer),
                  pl.BlockSpec((tm, D_MODEL), row),
                  _const_spec((1, D_MODEL)),
                  _const_spec((1, D_MODEL))],
        out_specs=[pl.BlockSpec((tm, D_MODEL), row), pl.BlockSpec((tm, D_MODEL), row)],
        out_shape=[jax.ShapeDtypeStruct((n, D_MODEL), F32),
                   jax.ShapeDtypeStruct((n, D_MODEL), BF16)],
        compiler_params=_cparams("parallel"),
        name="pool_mixer",
    )(u, u, w_in, w_grp, scale, w_out, h, ga, gb)


def _gelu_tanh(x):
    c = math.sqrt(2.0 / math.pi)
    return 0.5 * x * (1.0 + jnp.tanh(c * (x + 0.044715 * (x * x * x))))


def _sgu_kernel(u_ref, win_ref, lng_ref, lnb_ref, ws_ref, bst_ref, wout_ref, h_ref, ga_ref, gb_ref,
                ho_ref, uo_ref, gated_ref):
    tm = u_ref.shape[0]
    y = _gelu_tanh(_dot(u_ref[...], win_ref[...]))
    uu = y[:, :D_MODEL]
    v = y[:, D_MODEL:]
    mu = jnp.mean(v, axis=-1, keepdims=True)
    vc = v - mu
    var = jnp.mean(vc * vc, axis=-1, keepdims=True)
    vn = (vc * lax.rsqrt(var + LN_EPS) * lng_ref[...] + lnb_ref[...]).astype(BF16)
    t_idx = lax.broadcasted_iota(jnp.int32, (SGU_CHUNK, SGU_CHUNK), 0)
    s_idx = lax.broadcasted_iota(jnp.int32, (SGU_CHUNK, SGU_CHUNK), 1)
    causal = s_idx <= t_idx
    bst = bst_ref[...]
    for g in range(SGU_GROUPS):
        cols = slice(g * SGU_GROUP_DIM, (g + 1) * SGU_GROUP_DIM)
        ws = jnp.where(causal, ws_ref[g], 0.0).astype(BF16)
        bias = bst[:, g:g + 1]
        for ci in range(tm // SGU_CHUNK):
            rows = slice(ci * SGU_CHUNK, (ci + 1) * SGU_CHUNK)
            mixed = _dot(ws, vn[rows, cols]) + bias
            gated_ref[rows, cols] = (uu[rows, cols] * mixed).astype(BF16)
    _residual_update(_dot(gated_ref[...], wout_ref[...]), h_ref, ga_ref, gb_ref, ho_ref, uo_ref)


def _sgu_mixer(u, h, w_in, ln_g, ln_b, w_s, b_s_t, w_out, ga, gb):
    n = u.shape[0]
    tm = MIX_ROW_TILE
    row = lambda i: (i, 0)
    return pl.pallas_call(
        _sgu_kernel,
        grid=(n // tm,),
        in_specs=[pl.BlockSpec((tm, D_MODEL), row),
                  _const_spec((D_MODEL, 2 * D_MODEL)),
                  _const_spec((1, D_MODEL)),
                  _const_spec((1, D_MODEL)),
                  _const_spec(w_s.shape),
                  _const_spec(b_s_t.shape),
                  _const_spec((D_MODEL, D_MODEL)),
                  pl.BlockSpec((tm, D_MODEL), row),
                  _const_spec((1, D_MODEL)),
                  _const_spec((1, D_MODEL))],
        out_specs=[pl.BlockSpec((tm, D_MODEL), row), pl.BlockSpec((tm, D_MODEL), row)],
        out_shape=[jax.ShapeDtypeStruct((n, D_MODEL), F32),
                   jax.ShapeDtypeStruct((n, D_MODEL), BF16)],
        scratch_shapes=[pltpu.VMEM((tm, D_MODEL), BF16)],
        compiler_params=_cparams("parallel"),
        name="sgu_mixer",
    )(u, w_in, ln_g, ln_b, w_s, b_s_t, w_out, h, ga, gb)


def _nsa_proj_kernel(u_ref, w_ref, bias_ref, oscale_ref, cos_ref, sin_ref, o_ref, *, rope_chunks,
                     gate_chunks):
    j = pl.program_id(1)
    acc = _dot(u_ref[...], w_ref[...])
    tn = acc.shape[1]

    def is_in(chunks):
        hit = j == chunks[0]
        for c in chunks[1:]:
            hit = hit | (j == c)
        return hit

    rope = is_in(rope_chunks)
    plain = jnp.logical_not(rope)
    if gate_chunks:
        gate = is_in(gate_chunks)
        plain = plain & jnp.logical_not(gate)

        @pl.when(gate)
        def _():
            o_ref[...] = (1.0 / (1.0 + jnp.exp(-(acc + bias_ref[...])))).astype(o_ref.dtype)

    @pl.when(rope)
    def _():
        cos = cos_ref[...]
        sin = sin_ref[...]
        for hh in range(tn // HEAD_DIM):
            cols = slice(hh * HEAD_DIM, (hh + 1) * HEAD_DIM)
            x = acc[:, cols]
            roped = x * cos + pltpu.roll(x, HEAD_DIM // 2, 1) * sin
            o_ref[:, cols] = (roped * oscale_ref[:, cols]).astype(o_ref.dtype)

    @pl.when(plain)
    def _():
        o_ref[...] = acc.astype(o_ref.dtype)


def _nsa_proj(u, w, bias, oscale, cos, sin, *, tn, out_dtype, rope_chunks, gate_chunks, seq):
    n = u.shape[0]
    ncols = w.shape[1]
    tm = PROJ_ROW_TILE
    seq_tiles = seq // tm
    return pl.pallas_call(
        functools.partial(_nsa_proj_kernel, rope_chunks=rope_chunks, gate_chunks=gate_chunks),
        grid=(n // tm, ncols // tn),
        in_specs=[pl.BlockSpec((tm, D_MODEL), lambda i, j: (i, 0)),
                  pl.BlockSpec((D_MODEL, tn), lambda i, j: (0, j)),
                  pl.BlockSpec((1, tn), lambda i, j: (0, j)),
                  pl.BlockSpec((1, tn), lambda i, j: (0, j)),
                  pl.BlockSpec((tm, HEAD_DIM), lambda i, j: (i % seq_tiles, 0)),
                  pl.BlockSpec((tm, HEAD_DIM), lambda i, j: (i % seq_tiles, 0))],
        out_specs=pl.BlockSpec((tm, tn), lambda i, j: (i, j)),
        out_shape=jax.ShapeDtypeStruct((n, ncols), out_dtype),
        compiler_params=_cparams("parallel", "arbitrary"),
        name="nsa_proj",
    )(u, w, bias, oscale, cos, sin)


def _nsa_vt_kernel(u_ref, w_ref, o_ref):
    acc = _dot(u_ref[...], w_ref[...])
    for kt in range(acc.shape[0] // ATT_TILE):
        for g in range(N_KV_HEADS):
            tile = acc[kt * ATT_TILE:(kt + 1) * ATT_TILE, g * HEAD_DIM:(g + 1) * HEAD_DIM]
            o_ref[kt, g] = tile.T.astype(o_ref.dtype)


def _nsa_vt(u, w):
    n = u.shape[0]
    tm = PROJ_ROW_TILE
    kt = tm // ATT_TILE
    return pl.pallas_call(
        _nsa_vt_kernel,
        grid=(n // tm, 2),
        in_specs=[pl.BlockSpec((tm, D_MODEL), lambda i, j: (i, 0)),
                  pl.BlockSpec((D_MODEL, NSA_KV_DIM), lambda i, j: (0, j))],
        out_specs=pl.BlockSpec((None, kt, N_KV_HEADS, HEAD_DIM, ATT_TILE), lambda i, j: (j, i, 0, 0, 0)),
        out_shape=jax.ShapeDtypeStruct((2, n // ATT_TILE, N_KV_HEADS, HEAD_DIM, ATT_TILE), BF16),
        compiler_params=_cparams("parallel", "arbitrary"),
        name="nsa_vt",
    )(u, w)


def _compress_kernel(a_ref, pe_ref, w1_ref, b1_ref, w2_ref, b2_ref, o_ref):
    nb = a_ref.shape[0] // CMP_STRIDE
    top = jnp.zeros((nb, CMP_HIDDEN), F32)
    bot = jnp.zeros((nb, CMP_HIDDEN), F32)
    for p in range(CMP_STRIDE):
        xp = a_ref[pl.ds(p, nb, stride=CMP_STRIDE), :]
        lo = slice(p * HEAD_DIM, (p + 1) * HEAD_DIM)
        hi = slice((CMP_STRIDE + p) * HEAD_DIM, (CMP_STRIDE + p + 1) * HEAD_DIM)
        top += _dot((xp + pe_ref[p:p + 1, :]).astype(BF16), w1_ref[lo, :].astype(BF16))
        bot += _dot((xp + pe_ref[CMP_STRIDE + p:CMP_STRIDE + p + 1, :]).astype(BF16),
                    w1_ref[hi, :].astype(BF16))
    hid = _gelu_tanh(top + pltpu.roll(bot, nb - 1, 0) + b1_ref[...])
    res = _dot(hid.astype(BF16), w2_ref[...].astype(BF16)) + b2_ref[...]

    @pl.when(pl.program_id(1) == 0)
    def _():
        o_ref[...] = res.astype(o_ref.dtype)

    @pl.when(pl.program_id(1) == 1)
    def _():
        o_ref[...] = res.T.astype(o_ref.dtype)


def _compress(cv, pe, w1, b1, w2, b2, batch, seq):
    nb = seq // CMP_STRIDE
    return pl.pallas_call(
        _compress_kernel,
        grid=(batch, 2, N_KV_HEADS),
        in_specs=[pl.BlockSpec((seq, HEAD_DIM), lambda b, s, g: (b, s * N_KV_HEADS + g)),
                  pl.BlockSpec((None, CMP_BLOCK, HEAD_DIM), lambda b, s, g: (s, 0, 0)),
                  pl.BlockSpec((None, CMP_BLOCK * HEAD_DIM, CMP_HIDDEN), lambda b, s, g: (s, 0, 0)),
                  pl.BlockSpec((None, 1, CMP_HIDDEN), lambda b, s, g: (s, 0, 0)),
                  pl.BlockSpec((None, CMP_HIDDEN, HEAD_DIM), lambda b, s, g: (s, 0, 0)),
                  pl.BlockSpec((None, 1, HEAD_DIM), lambda b, s, g: (s, 0, 0))],
        out_specs=pl.BlockSpec((None, None, None, nb, HEAD_DIM), lambda b, s, g: (b, s, g, 0, 0)),
        out_shape=jax.ShapeDtypeStruct((batch, 2, N_KV_HEADS, nb, HEAD_DIM), BF16),
        compiler_params=_cparams("parallel", "parallel", "parallel"),
        name="nsa_compress",
    )(cv, pe, w1, b1, w2, b2)


def _attn_update(carry, qt, branches, tq, slab):
    m, l, acc = carry
    ms, ls, accs = [], [], []
    for b, (k, vt, fix) in enumerate(branches):
        for r in range(GQA_REP):
            for lo in range(0, tq, slab):
                q0 = r * tq + lo
                c0 = b * GQA_REP * tq + q0
                z = fix(_dot(k, qt[:, q0:q0 + slab]), lo)
                m_old = m[:, c0:c0 + slab]
                m_new = jnp.maximum(m_old, jnp.max(z, axis=0, keepdims=True))
                alpha = jnp.exp2(m_old - m_new)
                p = jnp.exp2(z - m_new)
                ms.append(m_new)
                ls.append(alpha * l[:, c0:c0 + slab] + jnp.sum(p, axis=0, keepdims=True))
                accs.append(alpha * acc[:, c0:c0 + slab] + _dot(vt, p.astype(BF16)))
    return tuple(jnp.concatenate(x, axis=1) for x in (ms, ls, accs))


def _nsa_attn_kernel(q_ref, ks_ref, kw_ref, vst_ref, vwt_ref, kc_ref, vct_ref, gate_ref, ovt_ref,
                     o_ref, bias_ref, *, n_cmp):
    i = pl.program_id(2)
    tq = q_ref.shape[0]
    tk = ATT_K_STEP
    nq = GQA_REP * tq
    n_slc, n_win = ovt_ref.shape

    q = q_ref[...].astype(F32)
    qt = jnp.concatenate([q[:, r * HEAD_DIM:(r + 1) * HEAD_DIM].T for r in range(GQA_REP)],
                         axis=1).astype(BF16)
    rel = lax.broadcasted_iota(jnp.int32, (tk, tq), 1) - lax.broadcasted_iota(jnp.int32, (tk, tq), 0)

    win_idx = lax.broadcasted_iota(jnp.int32, (n_win, nq), 0)
    t_abs = i * tq + (lax.broadcasted_iota(jnp.int32, (n_win, nq), 1) & (tq - 1))
    sc = _dot(kc_ref[...], qt)
    ok_c = (win_idx * CMP_STRIDE + (CMP_BLOCK - 1) <= t_abs) & (win_idx < n_cmp)
    sc = jnp.where(ok_c, sc, NEG)
    e = jnp.exp2(sc - jnp.max(sc, axis=0, keepdims=True))
    p_c = e / jnp.sum(e, axis=0, keepdims=True)
    p_c = jnp.where(t_abs >= CMP_BLOCK - 1, p_c, 0.0)
    o_c = _dot(vct_ref[...], p_c.astype(BF16))

    p_sum = p_c[:, 0:tq]
    for r in range(1, GQA_REP):
        p_sum = p_sum + p_c[:, r * tq:(r + 1) * tq]
    p_hi = p_sum.astype(BF16)
    p_lo = (p_sum - p_hi.astype(F32)).astype(BF16)
    ovt = ovt_ref[...]
    imp = _dot(ovt, p_hi) + _dot(ovt, p_lo)
    blk = lax.broadcasted_iota(jnp.int32, (n_slc, tq), 0)
    cur = (i * tq + lax.broadcasted_iota(jnp.int32, (n_slc, tq), 1)) >> int(math.log2(SLC_BLOCK))
    forced = (blk == 0) | (blk == cur) | (blk == cur - 1)
    imp = jnp.where(forced, BIG, imp)
    imp = jnp.where(blk <= cur, imp, NEG)
    rank = jnp.zeros((n_slc, tq), jnp.int32)
    for k in range(n_slc):
        other = imp[k:k + 1, :]
        ahead = (other > imp) | ((other == imp) & (blk > k))
        rank = rank + ahead.astype(jnp.int32)
    bias_ref[...] = jnp.where((rank < SLC_TOPK) & (blk <= cur), 0.0, NEG)

    init = (jnp.full((1, nq), NEG, F32), jnp.zeros((1, nq), F32), jnp.zeros((HEAD_DIM, nq), F32))
    blocks_per_tile = tk // SLC_BLOCK
    tiles_per_step = tk // ATT_TILE
    diag0 = i * (tq // tk)
    band0 = jnp.maximum(diag0 - WIN // tk, 0)
    slab = LANES

    def keys(ref, kt):
        return ref[pl.ds(pl.multiple_of(kt * tk, tk), tk), :]

    def values_t(ref, kt):
        return jnp.concatenate([ref[tiles_per_step * kt + j] for j in range(tiles_per_step)], axis=1)

    def block_bias(kt):
        rows = [jnp.broadcast_to(bias_ref[pl.ds(blocks_per_tile * kt + j, 1), :], (SLC_BLOCK, tq))
                for j in range(blocks_per_tile)]
        return jnp.concatenate(rows, axis=0)

    def far_body(kt, c):
        bias = block_bias(kt)
        slc = (keys(ks_ref, kt), values_t(vst_ref, kt), lambda z, lo: z + bias[:, lo:lo + slab])
        return _attn_update(c, qt, [slc], tq, slab)

    def near_body(kt, c):
        bias = block_bias(kt)
        in_band = rel < WIN - (i * tq - kt * tk)
        slc = (keys(ks_ref, kt), values_t(vst_ref, kt), lambda z, lo: z + bias[:, lo:lo + slab])
        win = (keys(kw_ref, kt), values_t(vwt_ref, kt),
               lambda z, lo: jnp.where(in_band[:, lo:lo + slab], z, NEG))
        return _attn_update(c, qt, [slc, win], tq, slab)

    c_s = lax.fori_loop(0, band0, far_body, init)
    both = tuple(jnp.concatenate([s, w], axis=1) for s, w in zip(c_s, init))
    both = lax.fori_loop(band0, diag0, near_body, both)
    for j in range(tq // tk):
        kt = diag0 + j
        bias = block_bias(kt)
        causal = rel >= j * tk
        slc = (keys(ks_ref, kt), values_t(vst_ref, kt),
               lambda z, lo: jnp.where(causal[:, lo:lo + slab], z + bias[:, lo:lo + slab], NEG))
        win = (keys(kw_ref, kt), values_t(vwt_ref, kt),
               lambda z, lo: jnp.where(causal[:, lo:lo + slab], z, NEG))
        both = _attn_update(both, qt, [slc, win], tq, slab)
    _, l_sw, a_sw = both
    o_sw = a_sw / l_sw
    o_s = o_sw[:, :nq]
    o_w = o_sw[:, nq:]

    gt = gate_ref[...].T

    def gate(k):
        return jnp.concatenate([gt[r * N_GATES + k:r * N_GATES + k + 1, :] for r in range(GQA_REP)],
                               axis=1)

    o_t = gate(0) * o_c + gate(1) * o_s + gate(2) * o_w
    for r in range(GQA_REP):
        o_ref[:, r * HEAD_DIM:(r + 1) * HEAD_DIM] = o_t[:, r * tq:(r + 1) * tq].T.astype(o_ref.dtype)


def _nsa_attention(qk, vt, cmp, cvg, ovt, batch, seq):
    n = qk.shape[0]
    tq = ATT_Q_TILE
    assert tq % ATT_K_STEP == 0 and tq <= WIN and WIN % ATT_K_STEP == 0
    assert ATT_K_STEP % ATT_TILE == 0 and ATT_TILE % SLC_BLOCK == 0
    qt = seq // tq
    gw = GQA_REP * HEAD_DIM
    k0 = NSA_Q_DIM // HEAD_DIM
    gate0 = 2 * N_KV_HEADS
    n_win = seq // CMP_STRIDE

    def k_spec(which):
        return pl.BlockSpec((seq, HEAD_DIM), lambda b, g, i: (b, k0 + which * N_KV_HEADS + g))

    def vt_spec(which):
        return pl.BlockSpec((None, seq // ATT_TILE, None, HEAD_DIM, ATT_TILE),
                            lambda b, g, i: (which, b, g, 0, 0))

    def cmp_spec(which):
        return pl.BlockSpec((None, None, None, n_win, HEAD_DIM), lambda b, g, i: (b, which, g, 0, 0))

    return pl.pallas_call(
        functools.partial(_nsa_attn_kernel, n_cmp=n_win - CMP_BLOCK // CMP_STRIDE + 1),
        grid=(batch, N_KV_HEADS, qt),
        in_specs=[pl.BlockSpec((tq, gw), lambda b, g, i: (b * qt + i, g)),
                  k_spec(0), k_spec(1), vt_spec(0), vt_spec(1),
                  cmp_spec(0), cmp_spec(1),
                  pl.BlockSpec((tq, LANES), lambda b, g, i: (b * qt + i, gate0 + g)),
                  pl.BlockSpec(ovt.shape, lambda b, g, i: (0, 0))],
        out_specs=pl.BlockSpec((tq, gw), lambda b, g, i: (b * qt + i, g)),
        out_shape=jax.ShapeDtypeStruct((n, NSA_Q_DIM), BF16),
        scratch_shapes=[pltpu.VMEM(ovt.shape[:1] + (tq,), F32)],
        compiler_params=_cparams("parallel", "parallel", "arbitrary"),
        name="nsa_attention",
    )(qk, qk, qk, vt, vt, cmp, cmp, cvg, ovt)


def _rope_tables(seq):
    half = HEAD_DIM // 2
    inv = 1.0 / (ROPE_THETA ** (jnp.arange(half, dtype=F32) / half))
    ang = jnp.arange(seq, dtype=F32)[:, None] * inv[None, :]
    cos = jnp.cos(ang)
    sin = jnp.sin(ang)
    return jnp.concatenate([cos, cos], axis=1), jnp.concatenate([-sin, sin], axis=1)


def _overlap_matrix(seq):
    n_win = seq // CMP_STRIDE
    n_cmp = n_win - CMP_BLOCK // CMP_STRIDE + 1
    sj = np.arange(seq // SLC_BLOCK)[:, None]
    ci = np.arange(n_win)[None, :]
    ov = ((ci * CMP_STRIDE <= (sj + 1) * SLC_BLOCK - 1)
          & (ci * CMP_STRIDE + CMP_BLOCK - 1 >= sj * SLC_BLOCK) & (ci < n_cmp))
    return jnp.asarray(ov, BF16)


def _nsa_mixer(u, h, w_in, gate_b, cmp_pe, cmp_w1, cmp_b1, cmp_w2, cmp_b2, w_out, ga, gb, batch, seq):
    assert seq // CMP_STRIDE == ATT_TILE and seq % PROJ_ROW_TILE == 0
    q0 = NSA_Q_DIM
    kvd = NSA_KV_DIM

    def part(k):
        return w_in[:, q0 + k * kvd:q0 + (k + 1) * kvd]

    w_qk = jnp.concatenate([w_in[:, :q0], part(2), part(4)], axis=1).astype(BF16)
    w_v = jnp.concatenate([part(3), part(5)], axis=1).astype(BF16)
    per_group = GQA_REP * N_GATES
    w_g = w_in[:, q0 + 6 * kvd:].reshape(D_MODEL, N_KV_HEADS, per_group)
    w_g = jnp.pad(w_g, ((0, 0), (0, 0), (0, LANES - per_group))).reshape(D_MODEL, N_KV_HEADS * LANES)
    b_g = jnp.pad(gate_b.reshape(N_KV_HEADS, per_group), ((0, 0), (0, LANES - per_group)))
    w_c = jnp.concatenate([part(0), part(1), w_g], axis=1).astype(BF16)
    bias_c = jnp.concatenate([jnp.zeros((2 * kvd,), F32), b_g.reshape(-1)])[None, :]
    cos, sin = _rope_tables(seq)

    tn = kvd
    q_scale = HEAD_DIM ** -0.5 * math.log2(math.e)
    oscale_qk = jnp.concatenate([jnp.full((q0,), q_scale, F32), jnp.ones((2 * kvd,), F32)])[None, :]
    qk = _nsa_proj(u, w_qk, jnp.zeros((1, w_qk.shape[1]), F32), oscale_qk, cos, sin, tn=tn, out_dtype=BF16,
                   rope_chunks=tuple(range(w_qk.shape[1] // tn)), gate_chunks=(), seq=seq)
    vt = _nsa_vt(u, w_v)
    cvg = _nsa_proj(u, w_c, bias_c, jnp.ones_like(bias_c), cos, sin, tn=tn, out_dtype=F32,
                    rope_chunks=(0,), gate_chunks=(2,), seq=seq)
    cmp = _compress(cvg, cmp_pe, cmp_w1, cmp_b1[:, None, :], cmp_w2, cmp_b2[:, None, :], batch, seq)
    o = _nsa_attention(qk, vt, cmp, cvg, _overlap_matrix(seq), batch, seq)
    return _out_proj(o, w_out.astype(BF16), h, ga, gb)


def kernel(x, norm_g, ffn_w_up, ffn_conv_w, ffn_conv_b, ffn_w_down, pool_w_in, pool_w_grp, pool_scale,
           pool_w_out, sgu_w_in, sgu_ln_g, sgu_ln_b, sgu_w_s, sgu_b_s, sgu_w_out, nsa_w_in, nsa_gate_b,
           nsa_cmp_pe, nsa_cmp_w1, nsa_cmp_b1, nsa_cmp_w2, nsa_cmp_b2, nsa_w_out):
    batch, seq, d = x.shape
    h = x.reshape(batch * seq, d)

    def gain(i, k):
        return norm_g[i, k][None, :]

    ffn_up, ffn_down = ffn_w_up.astype(BF16), ffn_w_down.astype(BF16)
    pool_in, pool_grp, pool_out = pool_w_in.astype(BF16), pool_w_grp.astype(BF16), pool_w_out.astype(BF16)
    gains = norm_g[:, :, None, :]
    conv_b = ffn_conv_b[:, None, :]

    u = _first_norm(h, gain(0, 0))
    for i in range(DEPTH):
        kind, j = i % 3, i // 3
        ga, gb = gain(i, 1), gain(i, 2)
        if kind == 0:
            h, u = _pool_mixer(u, h, pool_in, pool_grp, pool_scale[j][None, :], pool_out, ga, gb, j, seq)
        elif kind == 1:
            h, u = _sgu_mixer(u, h, sgu_w_in[j].astype(BF16), sgu_ln_g[j][None, :], sgu_ln_b[j][None, :],
                              sgu_w_s[j], sgu_b_s[j].T, sgu_w_out[j].astype(BF16), ga, gb)
        else:
            h, u = _nsa_mixer(u, h, nsa_w_in[j], nsa_gate_b[j], nsa_cmp_pe[j], nsa_cmp_w1[j],
                              nsa_cmp_b1[j], nsa_cmp_w2[j], nsa_cmp_b2[j], nsa_w_out[j], ga, gb,
                              batch, seq)
        next_gain = (i + 1, 0) if i + 1 < DEPTH else (i, 3)
        h, u = _conv_ffn(u, h, ffn_up, ffn_conv_w, conv_b, ffn_down, gains, i, next_gain, seq)
    return h.reshape(batch, seq, d)
```

```python
import functools
import math

import jax
import jax.numpy as jnp
import numpy as np
from jax import lax
from jax.experimental import pallas as pl
from jax.experimental.pallas import tpu as pltpu

F32 = jnp.float32
BF16 = jnp.bfloat16

D_MODEL = 2048
DEPTH = 4
RMS_EPS = 1e-6
LN_EPS = 1e-5
NEG = -1e30
BIG = 1e30

FFN_DIM = 5632
CONV_WIDTH = 3
POOL_WINDOWS = (2, 4, 8, 16)
POOL_GROUP_DIM = D_MODEL // len(POOL_WINDOWS)
SGU_CHUNK = 128
SGU_GROUPS = 16
SGU_GROUP_DIM = D_MODEL // SGU_GROUPS

HEAD_DIM = 128
N_HEADS = 16
N_KV_HEADS = 4
GQA_REP = N_HEADS // N_KV_HEADS
ROPE_THETA = 10000.0
CMP_BLOCK = 32
CMP_STRIDE = 16
CMP_HIDDEN = 2 * HEAD_DIM
SLC_BLOCK = 64
SLC_TOPK = 16
WIN = 512
NSA_Q_DIM = N_HEADS * HEAD_DIM
NSA_KV_DIM = N_KV_HEADS * HEAD_DIM
N_GATES = 3

LANES = 128
SUBLANES = 8
BF16_SUBLANES = 16
VMEM_LIMIT = 56 * 1024 * 1024

ROW_TILE = 512
PROJ_ROW_TILE = 1024
MIX_ROW_TILE = 256
FFN_ROW_TILE = 1024
FFN_COL_TILE = 512
ATT_TILE = 128
ATT_K_STEP = 256
ATT_Q_TILE = 256


def _cparams(*sem):
    return pltpu.CompilerParams(dimension_semantics=sem, vmem_limit_bytes=VMEM_LIMIT)


def _const_spec(shape):
    nd = len(shape)
    return pl.BlockSpec(shape, lambda *_: (0,) * nd, pipeline_mode=pl.Buffered(1))


def _layer_spec(stacked_shape, layer):
    nd = len(stacked_shape) - 1
    return pl.BlockSpec((None,) + tuple(stacked_shape[1:]), lambda *_: (layer,) + (0,) * nd,
                        pipeline_mode=pl.Buffered(1))


def _rms(x):
    return x * lax.rsqrt(jnp.mean(x * x, axis=-1, keepdims=True) + RMS_EPS)


def _residual_update(m, h_ref, ga_ref, gb_ref, ho_ref, uo_ref):
    hn = h_ref[...] + _rms(m) * ga_ref[...]
    ho_ref[...] = hn
    uo_ref[...] = (_rms(hn) * gb_ref[...]).astype(BF16)


def _dot(a, b):
    return jnp.dot(a, b, preferred_element_type=F32)


def _out_proj_kernel(x_ref, w_ref, h_ref, ga_ref, gb_ref, ho_ref, uo_ref):
    _residual_update(_dot(x_ref[...], w_ref[...]), h_ref, ga_ref, gb_ref, ho_ref, uo_ref)


def _out_proj(xin, w, h, ga, gb):
    n, k = xin.shape
    tm = ROW_TILE
    row = lambda i: (i, 0)
    return pl.pallas_call(
        _out_proj_kernel,
        grid=(n // tm,),
        in_specs=[pl.BlockSpec((tm, k), row), _const_spec((k, D_MODEL)),
                  pl.BlockSpec((tm, D_MODEL), row),
                  pl.BlockSpec((1, D_MODEL), lambda i: (0, 0)),
                  pl.BlockSpec((1, D_MODEL), lambda i: (0, 0))],
        out_specs=[pl.BlockSpec((tm, D_MODEL), row), pl.BlockSpec((tm, D_MODEL), row)],
        out_shape=[jax.ShapeDtypeStruct((n, D_MODEL), F32),
                   jax.ShapeDtypeStruct((n, D_MODEL), BF16)],
        compiler_params=_cparams("parallel"),
        name="out_proj",
    )(xin, w, h, ga, gb)


def _ffn_kernel(u_ref, wa_ref, wb_ref, cw_ref, cb_ref, wd_ref, h_ref, ga_ref, gb_ref,
                ho_ref, uo_ref, tail_ref, *, seq_tiles):
    i = pl.program_id(0)
    c = pl.program_id(1)
    tm = u_ref.shape[0]
    keep = tail_ref.shape[1]

    @pl.when(i % seq_tiles == 0)
    def _():
        tail_ref[c] = jnp.zeros(tail_ref.shape[1:], F32)

    @pl.when(c == 0)
    def _():
        ho_ref[...] = jnp.zeros_like(ho_ref)

    u = u_ref[...]
    a = _dot(u, wa_ref[...])
    b = _dot(u, wb_ref[...])
    prev = tail_ref[c]
    tail_ref[c] = a[tm - keep:tm]
    row = lax.broadcasted_iota(jnp.int32, a.shape, 0)
    last1 = prev[keep - 1:keep]
    last2 = prev[keep - 2:keep - 1]
    p1 = jnp.where(row == 0, last1, pltpu.roll(a, 1, 0))
    p2 = jnp.where(row == 0, last2, jnp.where(row == 1, last1, pltpu.roll(a, 2, 0)))
    cw = cw_ref[...]
    y = cw[0:1] * p2 + cw[1:2] * p1 + cw[2:3] * a + cb_ref[...]
    gated = y / (1.0 + jnp.exp(-y)) * b
    ho_ref[...] += _dot(gated.astype(BF16), wd_ref[...])

    @pl.when(c == pl.num_programs(1) - 1)
    def _():
        _residual_update(ho_ref[...], h_ref, ga_ref, gb_ref, ho_ref, uo_ref)


def _conv_ffn(u, h, w_up, conv_w, conv_b, w_down, norm_g, layer, next_gain, seq):
    n = u.shape[0]
    tm, tf = FFN_ROW_TILE, FFN_COL_TILE
    n_chunks = FFN_DIM // tf
    row = lambda i, c: (i, 0)

    def row_spec(buffers=1):
        return pl.BlockSpec((tm, D_MODEL), row, pipeline_mode=pl.Buffered(buffers))

    def gain_spec(lyr, slot):
        return pl.BlockSpec((None, None, 1, D_MODEL), lambda i, c: (lyr, slot, 0, 0))

    return pl.pallas_call(
        functools.partial(_ffn_kernel, seq_tiles=seq // tm),
        grid=(n // tm, n_chunks),
        in_specs=[row_spec(2),
                  pl.BlockSpec((None, D_MODEL, tf), lambda i, c: (layer, 0, c)),
                  pl.BlockSpec((None, D_MODEL, tf), lambda i, c: (layer, 0, c + n_chunks)),
                  pl.BlockSpec((None, CONV_WIDTH, tf), lambda i, c: (layer, 0, c)),
                  pl.BlockSpec((None, 1, tf), lambda i, c: (layer, 0, c)),
                  pl.BlockSpec((None, tf, D_MODEL), lambda i, c: (layer, c, 0)),
                  row_spec(),
                  gain_spec(layer, 3),
                  gain_spec(*next_gain)],
        out_specs=[row_spec(), row_spec(2)],
        out_shape=[jax.ShapeDtypeStruct((n, D_MODEL), F32),
                   jax.ShapeDtypeStruct((n, D_MODEL), BF16)],
        scratch_shapes=[pltpu.VMEM((n_chunks, SUBLANES, tf), F32)],
        compiler_params=_cparams("arbitrary", "arbitrary"),
        name="conv_ffn",
    )(u, w_up, w_up, conv_w, conv_b, w_down, h, norm_g, norm_g)


def _pool_kernel(u_ref, win_ref, wgrp_ref, scale_ref, wout_ref, h_ref, ga_ref, gb_ref,
                 ho_ref, uo_ref, tail_ref, m_ref, *, seq_tiles, norm_input):
    i = pl.program_id(0)
    tm = h_ref.shape[0]
    halo = tail_ref.shape[0]
    assert halo >= max(POOL_WINDOWS) - 1

    @pl.when(i % seq_tiles == 0)
    def _():
        tail_ref[...] = jnp.zeros_like(tail_ref)

    u = (_rms(h_ref[...]) * u_ref[...]).astype(BF16) if norm_input else u_ref[...]
    tpos = (i % seq_tiles) * tm + lax.broadcasted_iota(jnp.int32, (tm, 1), 0)
    for g, w in enumerate(POOL_WINDOWS):
        cols = slice(g * POOL_GROUP_DIM, (g + 1) * POOL_GROUP_DIM)
        z = _dot(u, win_ref[:, cols])
        x = jnp.concatenate([tail_ref[:, cols], z], axis=0)
        tail_ref[:, cols] = z[tm - halo:]
        s = x
        k = 1
        while k < w:
            s = s + pltpu.roll(s, k, 0)
            k *= 2
        cnt = jnp.minimum(tpos + 1, w).astype(F32)
        p = s[halo:] / cnt - z
        mg = _dot(p.astype(BF16), wgrp_ref[g]) * scale_ref[:, cols]
        m_ref[:, cols] = mg.astype(BF16)
    _residual_update(_dot(m_ref[...], wout_ref[...]), h_ref, ga_ref, gb_ref, ho_ref, uo_ref)


def _pool_mixer(u, h, w_in, w_grp, scale, w_out, ga, gb, layer, seq):
    n = h.shape[0]
    tm = ROW_TILE
    row = lambda i: (i, 0)
    norm_input = u.shape[0] == 1
    u_spec = _const_spec((1, D_MODEL)) if norm_input else pl.BlockSpec((tm, D_MODEL), row)
    return pl.pallas_call(
        functools.partial(_pool_kernel, seq_tiles=seq // tm, norm_input=norm_input),
        grid=(n // tm,),
        in_specs=[u_spec,
                  _layer_spec(w_in.shape, layer),
                  _layer_spec(w_grp.shape, layer),
                  _const_spec((1, D_MODEL)),
                  _layer_spec(w_out.shape, layer),
                  pl.BlockSpec((tm, D_MODEL), row),
                  _const_spec((1, D_MODEL)),
                  _const_spec((1, D_MODEL))],
        out_specs=[pl.BlockSpec((tm, D_MODEL), row), pl.BlockSpec((tm, D_MODEL), row)],
        out_shape=[jax.ShapeDtypeStruct((n, D_MODEL), F32),
                   jax.ShapeDtypeStruct((n, D_MODEL), BF16)],
        scratch_shapes=[pltpu.VMEM((BF16_SUBLANES, D_MODEL), F32), pltpu.VMEM((tm, D_MODEL), BF16)],
        compiler_params=_cparams("arbitrary"),
        name="pool_mixer",
    )(u, w_in, w_grp, scale, w_out, h, ga, gb)


def _gelu_tanh(x):
    c = math.sqrt(2.0 / math.pi)
    return 0.5 * x * (1.0 + jnp.tanh(c * (x + 0.044715 * (x * x * x))))


def _sgu_kernel(u_ref, win_ref, lng_ref, lnb_ref, ws_ref, bst_ref, wout_ref, h_ref, ga_ref, gb_ref,
                ho_ref, uo_ref, gated_ref):
    tm = u_ref.shape[0]
    y = _gelu_tanh(_dot(u_ref[...], win_ref[...]))
    uu = y[:, :D_MODEL]
    v = y[:, D_MODEL:]
    mu = jnp.mean(v, axis=-1, keepdims=True)
    vc = v - mu
    var = jnp.mean(vc * vc, axis=-1, keepdims=True)
    vn = (vc * lax.rsqrt(var + LN_EPS) * lng_ref[...] + lnb_ref[...]).astype(BF16)
    t_idx = lax.broadcasted_iota(jnp.int32, (SGU_CHUNK, SGU_CHUNK), 0)
    s_idx = lax.broadcasted_iota(jnp.int32, (SGU_CHUNK, SGU_CHUNK), 1)
    causal = s_idx <= t_idx
    bst = bst_ref[...]
    for g in range(SGU_GROUPS):
        cols = slice(g * SGU_GROUP_DIM, (g + 1) * SGU_GROUP_DIM)
        ws = jnp.where(causal, ws_ref[g], 0.0).astype(BF16)
        bias = bst[:, g:g + 1]
        for ci in range(tm // SGU_CHUNK):
            rows = slice(ci * SGU_CHUNK, (ci + 1) * SGU_CHUNK)
            mixed = _dot(ws, vn[rows, cols]) + bias
            gated_ref[rows, cols] = (uu[rows, cols] * mixed).astype(BF16)
    _residual_update(_dot(gated_ref[...], wout_ref[...]), h_ref, ga_ref, gb_ref, ho_ref, uo_ref)


def _sgu_mixer(u, h, w_in, ln_g, ln_b, w_s, b_s_t, w_out, ga, gb):
    n = u.shape[0]
    tm = MIX_ROW_TILE
    row = lambda i: (i, 0)
    return pl.pallas_call(
        _sgu_kernel,
        grid=(n // tm,),
        in_specs=[pl.BlockSpec((tm, D_MODEL), row),
                  _const_spec((D_MODEL, 2 * D_MODEL)),
                  _const_spec((1, D_MODEL)),
                  _const_spec((1, D_MODEL)),
                  _const_spec(w_s.shape),
                  _const_spec(b_s_t.shape),
                  _const_spec((D_MODEL, D_MODEL)),
                  pl.BlockSpec((tm, D_MODEL), row),
                  _const_spec((1, D_MODEL)),
                  _const_spec((1, D_MODEL))],
        out_specs=[pl.BlockSpec((tm, D_MODEL), row), pl.BlockSpec((tm, D_MODEL), row)],
        out_shape=[jax.ShapeDtypeStruct((n, D_MODEL), F32),
                   jax.ShapeDtypeStruct((n, D_MODEL), BF16)],
        scratch_shapes=[pltpu.VMEM((tm, D_MODEL), BF16)],
        compiler_params=_cparams("parallel"),
        name="sgu_mixer",
    )(u, w_in, ln_g, ln_b, w_s, b_s_t, w_out, h, ga, gb)


def _nsa_proj_kernel(u_ref, w_ref, bias_ref, oscale_ref, cos_ref, sin_ref, o_ref, *, rope_chunks,
                     gate_chunks):
    j = pl.program_id(1)
    acc = _dot(u_ref[...], w_ref[...])
    tn = acc.shape[1]

    def is_in(chunks):
        hit = j == chunks[0]
        for c in chunks[1:]:
            hit = hit | (j == c)
        return hit

    rope = is_in(rope_chunks)
    plain = jnp.logical_not(rope)
    if gate_chunks:
        gate = is_in(gate_chunks)
        plain = plain & jnp.logical_not(gate)

        @pl.when(gate)
        def _():
            o_ref[...] = (1.0 / (1.0 + jnp.exp(-(acc + bias_ref[...])))).astype(o_ref.dtype)

    @pl.when(rope)
    def _():
        cos = cos_ref[...]
        sin = sin_ref[...]
        for hh in range(tn // HEAD_DIM):
            cols = slice(hh * HEAD_DIM, (hh + 1) * HEAD_DIM)
            x = acc[:, cols]
            roped = x * cos + pltpu.roll(x, HEAD_DIM // 2, 1) * sin
            o_ref[:, cols] = (roped * oscale_ref[:, cols]).astype(o_ref.dtype)

    @pl.when(plain)
    def _():
        o_ref[...] = acc.astype(o_ref.dtype)


def _nsa_proj(u, w, bias, oscale, cos, sin, *, tn, out_dtype, rope_chunks, gate_chunks, seq):
    n = u.shape[0]
    ncols = w.shape[1]
    tm = PROJ_ROW_TILE
    seq_tiles = seq // tm
    return pl.pallas_call(
        functools.partial(_nsa_proj_kernel, rope_chunks=rope_chunks, gate_chunks=gate_chunks),
        grid=(n // tm, ncols // tn),
        in_specs=[pl.BlockSpec((tm, D_MODEL), lambda i, j: (i, 0)),
                  pl.BlockSpec((D_MODEL, tn), lambda i, j: (0, j)),
                  pl.BlockSpec((1, tn), lambda i, j: (0, j)),
                  pl.BlockSpec((1, tn), lambda i, j: (0, j)),
                  pl.BlockSpec((tm, HEAD_DIM), lambda i, j: (i % seq_tiles, 0)),
                  pl.BlockSpec((tm, HEAD_DIM), lambda i, j: (i % seq_tiles, 0))],
        out_specs=pl.BlockSpec((tm, tn), lambda i, j: (i, j)),
        out_shape=jax.ShapeDtypeStruct((n, ncols), out_dtype),
        compiler_params=_cparams("parallel", "arbitrary"),
        name="nsa_proj",
    )(u, w, bias, oscale, cos, sin)


def _nsa_vt_kernel(u_ref, w_ref, o_ref):
    acc = _dot(u_ref[...], w_ref[...])
    for kt in range(acc.shape[0] // ATT_TILE):
        for g in range(N_KV_HEADS):
            tile = acc[kt * ATT_TILE:(kt + 1) * ATT_TILE, g * HEAD_DIM:(g + 1) * HEAD_DIM]
            o_ref[kt, g] = tile.T.astype(o_ref.dtype)


def _nsa_vt(u, w):
    n = u.shape[0]
    tm = PROJ_ROW_TILE
    kt = tm // ATT_TILE
    return pl.pallas_call(
        _nsa_vt_kernel,
        grid=(n // tm, 2),
        in_specs=[pl.BlockSpec((tm, D_MODEL), lambda i, j: (i, 0)),
                  pl.BlockSpec((D_MODEL, NSA_KV_DIM), lambda i, j: (0, j))],
        out_specs=pl.BlockSpec((None, kt, N_KV_HEADS, HEAD_DIM, ATT_TILE), lambda i, j: (j, i, 0, 0, 0)),
        out_shape=jax.ShapeDtypeStruct((2, n // ATT_TILE, N_KV_HEADS, HEAD_DIM, ATT_TILE), BF16),
        compiler_params=_cparams("parallel", "arbitrary"),
        name="nsa_vt",
    )(u, w)


def _compress_kernel(a_ref, pe_ref, w1_ref, b1_ref, w2_ref, b2_ref, o_ref):
    nb = a_ref.shape[0] // CMP_STRIDE
    top = jnp.zeros((nb, CMP_HIDDEN), F32)
    bot = jnp.zeros((nb, CMP_HIDDEN), F32)
    for p in range(CMP_STRIDE):
        xp = a_ref[pl.ds(p, nb, stride=CMP_STRIDE), :]
        lo = slice(p * HEAD_DIM, (p + 1) * HEAD_DIM)
        hi = slice((CMP_STRIDE + p) * HEAD_DIM, (CMP_STRIDE + p + 1) * HEAD_DIM)
        top += _dot((xp + pe_ref[p:p + 1, :]).astype(BF16), w1_ref[lo, :].astype(BF16))
        bot += _dot((xp + pe_ref[CMP_STRIDE + p:CMP_STRIDE + p + 1, :]).astype(BF16),
                    w1_ref[hi, :].astype(BF16))
    hid = _gelu_tanh(top + pltpu.roll(bot, nb - 1, 0) + b1_ref[...])
    res = _dot(hid.astype(BF16), w2_ref[...].astype(BF16)) + b2_ref[...]

    @pl.when(pl.program_id(1) == 0)
    def _():
        o_ref[...] = res.astype(o_ref.dtype)

    @pl.when(pl.program_id(1) == 1)
    def _():
        o_ref[...] = res.T.astype(o_ref.dtype)


def _compress(cv, pe, w1, b1, w2, b2, batch, seq):
    nb = seq // CMP_STRIDE
    return pl.pallas_call(
        _compress_kernel,
        grid=(batch, 2, N_KV_HEADS),
        in_specs=[pl.BlockSpec((seq, HEAD_DIM), lambda b, s, g: (b, s * N_KV_HEADS + g)),
                  pl.BlockSpec((None, CMP_BLOCK, HEAD_DIM), lambda b, s, g: (s, 0, 0)),
                  pl.BlockSpec((None, CMP_BLOCK * HEAD_DIM, CMP_HIDDEN), lambda b, s, g: (s, 0, 0)),
                  pl.BlockSpec((None, 1, CMP_HIDDEN), lambda b, s, g: (s, 0, 0)),
                  pl.BlockSpec((None, CMP_HIDDEN, HEAD_DIM), lambda b, s, g: (s, 0, 0)),
                  pl.BlockSpec((None, 1, HEAD_DIM), lambda b, s, g: (s, 0, 0))],
        out_specs=pl.BlockSpec((None, None, None, nb, HEAD_DIM), lambda b, s, g: (b, s, g, 0, 0)),
        out_shape=jax.ShapeDtypeStruct((batch, 2, N_KV_HEADS, nb, HEAD_DIM), BF16),
        compiler_params=_cparams("parallel", "parallel", "parallel"),
        name="nsa_compress",
    )(cv, pe, w1, b1, w2, b2)


def _attn_update(carry, qt, branches, tq, slab):
    m, l, acc = carry
    ms, ls, accs = [], [], []
    for b, (k, vt, fix) in enumerate(branches):
        for r in range(GQA_REP):
            for lo in range(0, tq, slab):
                q0 = r * tq + lo
                c0 = b * GQA_REP * tq + q0
                z = fix(_dot(k, qt[:, q0:q0 + slab]), lo)
                m_old = m[:, c0:c0 + slab]
                m_new = jnp.maximum(m_old, jnp.max(z, axis=0, keepdims=True))
                alpha = jnp.exp2(m_old - m_new)
                p = jnp.exp2(z - m_new)
                ms.append(m_new)
                ls.append(alpha * l[:, c0:c0 + slab] + jnp.sum(p, axis=0, keepdims=True))
                accs.append(alpha * acc[:, c0:c0 + slab] + _dot(vt, p.astype(BF16)))
    return tuple(jnp.concatenate(x, axis=1) for x in (ms, ls, accs))


def _nsa_attn_kernel(q_ref, ks_ref, kw_ref, vst_ref, vwt_ref, kc_ref, vct_ref, gate_ref, ovt_ref,
                     o_ref, bias_ref, *, n_cmp):
    i = pl.program_id(2)
    tq = q_ref.shape[0]
    tk = ATT_K_STEP
    nq = GQA_REP * tq
    n_slc, n_win = ovt_ref.shape

    q = q_ref[...].astype(F32)
    qt = jnp.concatenate([q[:, r * HEAD_DIM:(r + 1) * HEAD_DIM].T for r in range(GQA_REP)],
                         axis=1).astype(BF16)
    rel = lax.broadcasted_iota(jnp.int32, (tk, tq), 1) - lax.broadcasted_iota(jnp.int32, (tk, tq), 0)

    win_idx = lax.broadcasted_iota(jnp.int32, (n_win, nq), 0)
    t_abs = i * tq + (lax.broadcasted_iota(jnp.int32, (n_win, nq), 1) & (tq - 1))
    sc = _dot(kc_ref[...], qt)
    ok_c = (win_idx * CMP_STRIDE + (CMP_BLOCK - 1) <= t_abs) & (win_idx < n_cmp)
    sc = jnp.where(ok_c, sc, NEG)
    e = jnp.exp2(sc - jnp.max(sc, axis=0, keepdims=True))
    p_c = e * (1.0 / jnp.sum(e, axis=0, keepdims=True))
    p_c = jnp.where(t_abs >= CMP_BLOCK - 1, p_c, 0.0)
    o_c = _dot(vct_ref[...], p_c.astype(BF16))

    p_sum = p_c[:, 0:tq]
    for r in range(1, GQA_REP):
        p_sum = p_sum + p_c[:, r * tq:(r + 1) * tq]
    p_hi = p_sum.astype(BF16)
    p_lo = (p_sum - p_hi.astype(F32)).astype(BF16)
    ovt = ovt_ref[...]
    imp = _dot(ovt, p_hi) + _dot(ovt, p_lo)
    blk = lax.broadcasted_iota(jnp.int32, (n_slc, tq), 0)
    cur = (i * tq + lax.broadcasted_iota(jnp.int32, (n_slc, tq), 1)) >> int(math.log2(SLC_BLOCK))
    forced = (blk == 0) | (blk == cur) | (blk == cur - 1)
    imp = jnp.where(forced, BIG, imp)
    imp = jnp.where(blk <= cur, imp, NEG)
    rank = jnp.zeros((n_slc, tq), jnp.int32)
    for k in range(n_slc):
        other = imp[k:k + 1, :]
        ahead = (other > imp) | ((other == imp) & (blk > k))
        rank = rank + ahead.astype(jnp.int32)
    bias_ref[...] = jnp.where((rank < SLC_TOPK) & (blk <= cur), 0.0, NEG)

    init = (jnp.full((1, nq), NEG, F32), jnp.zeros((1, nq), F32), jnp.zeros((HEAD_DIM, nq), F32))
    blocks_per_tile = tk // SLC_BLOCK
    tiles_per_step = tk // ATT_TILE
    diag0 = i * (tq // tk)
    band0 = jnp.maximum(diag0 - WIN // tk, 0)
    slab = LANES

    def keys(ref, kt):
        return ref[pl.ds(pl.multiple_of(kt * tk, tk), tk), :]

    def values_t(ref, kt):
        return jnp.concatenate([ref[tiles_per_step * kt + j] for j in range(tiles_per_step)], axis=1)

    def block_bias(kt):
        rows = [jnp.broadcast_to(bias_ref[pl.ds(blocks_per_tile * kt + j, 1), :], (SLC_BLOCK, tq))
                for j in range(blocks_per_tile)]
        return jnp.concatenate(rows, axis=0)

    def far_body(kt, c):
        bias = block_bias(kt)
        slc = (keys(ks_ref, kt), values_t(vst_ref, kt), lambda z, lo: z + bias[:, lo:lo + slab])
        return _attn_update(c, qt, [slc], tq, slab)

    def near_body(kt, c):
        bias = block_bias(kt)
        in_band = rel < WIN - (i * tq - kt * tk)
        slc = (keys(ks_ref, kt), values_t(vst_ref, kt), lambda z, lo: z + bias[:, lo:lo + slab])
        win = (keys(kw_ref, kt), values_t(vwt_ref, kt),
               lambda z, lo: jnp.where(in_band[:, lo:lo + slab], z, NEG))
        return _attn_update(c, qt, [slc, win], tq, slab)

    c_s = lax.fori_loop(0, band0, far_body, init)
    both = tuple(jnp.concatenate([s, w], axis=1) for s, w in zip(c_s, init))
    both = lax.fori_loop(band0, diag0, near_body, both)
    for j in range(tq // tk):
        kt = diag0 + j
        bias = block_bias(kt)
        causal = rel >= j * tk
        slc = (keys(ks_ref, kt), values_t(vst_ref, kt),
               lambda z, lo: jnp.where(causal[:, lo:lo + slab], z + bias[:, lo:lo + slab], NEG))
        win = (keys(kw_ref, kt), values_t(vwt_ref, kt),
               lambda z, lo: jnp.where(causal[:, lo:lo + slab], z, NEG))
        both = _attn_update(both, qt, [slc, win], tq, slab)
    _, l_sw, a_sw = both
    o_sw = a_sw * (1.0 / l_sw)
    o_s = o_sw[:, :nq]
    o_w = o_sw[:, nq:]

    gt = gate_ref[...].T

    def gate(k):
        return jnp.concatenate([gt[r * N_GATES + k:r * N_GATES + k + 1, :] for r in range(GQA_REP)],
                               axis=1)

    o_t = gate(0) * o_c + gate(1) * o_s + gate(2) * o_w
    for r in range(GQA_REP):
        o_ref[:, r * HEAD_DIM:(r + 1) * HEAD_DIM] = o_t[:, r * tq:(r + 1) * tq].T.astype(o_ref.dtype)


def _nsa_attention(qk, vt, cmp, cvg, ovt, batch, seq):
    n = qk.shape[0]
    tq = ATT_Q_TILE
    assert tq % ATT_K_STEP == 0 and tq <= WIN and WIN % ATT_K_STEP == 0
    assert ATT_K_STEP % ATT_TILE == 0 and ATT_TILE % SLC_BLOCK == 0
    qt = seq // tq
    gw = GQA_REP * HEAD_DIM
    k0 = NSA_Q_DIM // HEAD_DIM
    gate0 = 2 * N_KV_HEADS
    n_win = seq // CMP_STRIDE

    def k_spec(which):
        return pl.BlockSpec((seq, HEAD_DIM), lambda b, g, i: (b, k0 + which * N_KV_HEADS + g))

    def vt_spec(which):
        return pl.BlockSpec((None, seq // ATT_TILE, None, HEAD_DIM, ATT_TILE),
                            lambda b, g, i: (which, b, g, 0, 0))

    def cmp_spec(which):
        return pl.BlockSpec((None, None, None, n_win, HEAD_DIM), lambda b, g, i: (b, which, g, 0, 0))

    return pl.pallas_call(
        functools.partial(_nsa_attn_kernel, n_cmp=n_win - CMP_BLOCK // CMP_STRIDE + 1),
        grid=(batch, N_KV_HEADS, qt),
        in_specs=[pl.BlockSpec((tq, gw), lambda b, g, i: (b * qt + i, g)),
                  k_spec(0), k_spec(1), vt_spec(0), vt_spec(1),
                  cmp_spec(0), cmp_spec(1),
                  pl.BlockSpec((tq, LANES), lambda b, g, i: (b * qt + i, gate0 + g)),
                  pl.BlockSpec(ovt.shape, lambda b, g, i: (0, 0))],
        out_specs=pl.BlockSpec((tq, gw), lambda b, g, i: (b * qt + i, g)),
        out_shape=jax.ShapeDtypeStruct((n, NSA_Q_DIM), BF16),
        scratch_shapes=[pltpu.VMEM(ovt.shape[:1] + (tq,), F32)],
        compiler_params=_cparams("parallel", "parallel", "arbitrary"),
        name="nsa_attention",
    )(qk, qk, qk, vt, vt, cmp, cmp, cvg, ovt)


def _rope_tables(seq):
    half = HEAD_DIM // 2
    inv = 1.0 / (ROPE_THETA ** (jnp.arange(half, dtype=F32) / half))
    ang = jnp.arange(seq, dtype=F32)[:, None] * inv[None, :]
    cos = jnp.cos(ang)
    sin = jnp.sin(ang)
    return jnp.concatenate([cos, cos], axis=1), jnp.concatenate([-sin, sin], axis=1)


def _overlap_matrix(seq):
    n_win = seq // CMP_STRIDE
    n_cmp = n_win - CMP_BLOCK // CMP_STRIDE + 1
    sj = np.arange(seq // SLC_BLOCK)[:, None]
    ci = np.arange(n_win)[None, :]
    ov = ((ci * CMP_STRIDE <= (sj + 1) * SLC_BLOCK - 1)
          & (ci * CMP_STRIDE + CMP_BLOCK - 1 >= sj * SLC_BLOCK) & (ci < n_cmp))
    return jnp.asarray(ov, BF16)


def _nsa_mixer(u, h, w_in, gate_b, cmp_pe, cmp_w1, cmp_b1, cmp_w2, cmp_b2, w_out, ga, gb, batch, seq):
    assert seq // CMP_STRIDE == ATT_TILE and seq % PROJ_ROW_TILE == 0
    q0 = NSA_Q_DIM
    kvd = NSA_KV_DIM

    def part(k):
        return w_in[:, q0 + k * kvd:q0 + (k + 1) * kvd]

    w_qk = jnp.concatenate([w_in[:, :q0], part(2), part(4)], axis=1).astype(BF16)
    w_v = jnp.concatenate([part(3), part(5)], axis=1).astype(BF16)
    per_group = GQA_REP * N_GATES
    w_g = w_in[:, q0 + 6 * kvd:].reshape(D_MODEL, N_KV_HEADS, per_group)
    w_g = jnp.pad(w_g, ((0, 0), (0, 0), (0, LANES - per_group))).reshape(D_MODEL, N_KV_HEADS * LANES)
    b_g = jnp.pad(gate_b.reshape(N_KV_HEADS, per_group), ((0, 0), (0, LANES - per_group)))
    w_c = jnp.concatenate([part(0), part(1), w_g], axis=1).astype(BF16)
    bias_c = jnp.concatenate([jnp.zeros((2 * kvd,), F32), b_g.reshape(-1)])[None, :]
    cos, sin = _rope_tables(seq)

    tn = kvd
    q_scale = HEAD_DIM ** -0.5 * math.log2(math.e)
    oscale_qk = jnp.concatenate([jnp.full((q0,), q_scale, F32), jnp.ones((2 * kvd,), F32)])[None, :]
    qk = _nsa_proj(u, w_qk, jnp.zeros((1, w_qk.shape[1]), F32), oscale_qk, cos, sin, tn=tn, out_dtype=BF16,
                   rope_chunks=tuple(range(w_qk.shape[1] // tn)), gate_chunks=(), seq=seq)
    vt = _nsa_vt(u, w_v)
    cvg = _nsa_proj(u, w_c, bias_c, jnp.ones_like(bias_c), cos, sin, tn=tn, out_dtype=F32,
                    rope_chunks=(0,), gate_chunks=(2,), seq=seq)
    cmp = _compress(cvg, cmp_pe, cmp_w1, cmp_b1[:, None, :], cmp_w2, cmp_b2[:, None, :], batch, seq)
    o = _nsa_attention(qk, vt, cmp, cvg, _overlap_matrix(seq), batch, seq)
    return _out_proj(o, w_out.astype(BF16), h, ga, gb)


def kernel(x, norm_g, ffn_w_up, ffn_conv_w, ffn_conv_b, ffn_w_down, pool_w_in, pool_w_grp, pool_scale,
           pool_w_out, sgu_w_in, sgu_ln_g, sgu_ln_b, sgu_w_s, sgu_b_s, sgu_w_out, nsa_w_in, nsa_gate_b,
           nsa_cmp_pe, nsa_cmp_w1, nsa_cmp_b1, nsa_cmp_w2, nsa_cmp_b2, nsa_w_out):
    batch, seq, d = x.shape
    h = x.reshape(batch * seq, d)

    def gain(i, k):
        return norm_g[i, k][None, :]

    ffn_up, ffn_down = ffn_w_up.astype(BF16), ffn_w_down.astype(BF16)
    pool_in, pool_grp, pool_out = pool_w_in.astype(BF16), pool_w_grp.astype(BF16), pool_w_out.astype(BF16)
    gains = norm_g[:, :, None, :]
    conv_b = ffn_conv_b[:, None, :]

    u = gain(0, 0)
    for i in range(DEPTH):
        kind, j = i % 3, i // 3
        ga, gb = gain(i, 1), gain(i, 2)
        if kind == 0:
            h, u = _pool_mixer(u, h, pool_in, pool_grp, pool_scale[j][None, :], pool_out, ga, gb, j, seq)
        elif kind == 1:
            h, u = _sgu_mixer(u, h, sgu_w_in[j].astype(BF16), sgu_ln_g[j][None, :], sgu_ln_b[j][None, :],
                              sgu_w_s[j], sgu_b_s[j].T, sgu_w_out[j].astype(BF16), ga, gb)
        else:
            h, u = _nsa_mixer(u, h, nsa_w_in[j], nsa_gate_b[j], nsa_cmp_pe[j], nsa_cmp_w1[j],
                              nsa_cmp_b1[j], nsa_cmp_w2[j], nsa_cmp_b2[j], nsa_w_out[j], ga, gb,
                              batch, seq)
        next_gain = (i + 1, 0) if i + 1 < DEPTH else (i, 3)
        h, u = _conv_ffn(u, h, ffn_up, ffn_conv_w, conv_b, ffn_down, gains, i, next_gain, seq)
    return h.reshape(batch, seq, d)
```

```python
import functools
import math

import jax
import jax.numpy as jnp
import numpy as np
from jax import lax
from jax.experimental import pallas as pl
from jax.experimental.pallas import tpu as pltpu

F32 = jnp.float32
BF16 = jnp.bfloat16

D_MODEL = 2048
DEPTH = 4
RMS_EPS = 1e-6
LN_EPS = 1e-5
NEG = -1e30
BIG = 1e30

FFN_DIM = 5632
CONV_WIDTH = 3
POOL_WINDOWS = (2, 4, 8, 16)
POOL_GROUP_DIM = D_MODEL // len(POOL_WINDOWS)
SGU_CHUNK = 128
SGU_GROUPS = 16
SGU_GROUP_DIM = D_MODEL // SGU_GROUPS

HEAD_DIM = 128
N_HEADS = 16
N_KV_HEADS = 4
GQA_REP = N_HEADS // N_KV_HEADS
ROPE_THETA = 10000.0
CMP_BLOCK = 32
CMP_STRIDE = 16
CMP_HIDDEN = 2 * HEAD_DIM
SLC_BLOCK = 64
SLC_TOPK = 16
WIN = 512
NSA_Q_DIM = N_HEADS * HEAD_DIM
NSA_KV_DIM = N_KV_HEADS * HEAD_DIM
N_GATES = 3

LANES = 128
SUBLANES = 8
BF16_SUBLANES = 16
VMEM_LIMIT = 56 * 1024 * 1024
FFN_VMEM_LIMIT = 60 * 1024 * 1024

ROW_TILE = 512
PROJ_ROW_TILE = 1024
MIX_ROW_TILE = 256
FFN_ROW_TILE = 1024
FFN_COL_TILE = 512
ATT_TILE = 128
ATT_K_STEP = 256
ATT_Q_TILE = 256


def _cparams(*sem):
    return pltpu.CompilerParams(dimension_semantics=sem, vmem_limit_bytes=VMEM_LIMIT)


def _const_spec(shape):
    nd = len(shape)
    return pl.BlockSpec(shape, lambda *_: (0,) * nd, pipeline_mode=pl.Buffered(1))


def _layer_spec(stacked_shape, layer):
    nd = len(stacked_shape) - 1
    return pl.BlockSpec((None,) + tuple(stacked_shape[1:]), lambda *_: (layer,) + (0,) * nd,
                        pipeline_mode=pl.Buffered(1))


def _rms(x):
    return x * lax.rsqrt(jnp.mean(x * x, axis=-1, keepdims=True) + RMS_EPS)


def _residual_update(m, h_ref, ga_ref, gb_ref, ho_ref, uo_ref):
    hn = h_ref[...] + _rms(m) * ga_ref[...]
    ho_ref[...] = hn
    uo_ref[...] = (_rms(hn) * gb_ref[...]).astype(BF16)


def _dot(a, b):
    return jnp.dot(a, b, preferred_element_type=F32)


def _out_proj_kernel(x_ref, w_ref, h_ref, ga_ref, gb_ref, ho_ref, uo_ref):
    _residual_update(_dot(x_ref[...], w_ref[...]), h_ref, ga_ref, gb_ref, ho_ref, uo_ref)


def _out_proj(xin, w, h, ga, gb):
    n, k = xin.shape
    tm = ROW_TILE
    row = lambda i: (i, 0)
    return pl.pallas_call(
        _out_proj_kernel,
        grid=(n // tm,),
        in_specs=[pl.BlockSpec((tm, k), row), _const_spec((k, D_MODEL)),
                  pl.BlockSpec((tm, D_MODEL), row),
                  pl.BlockSpec((1, D_MODEL), lambda i: (0, 0)),
                  pl.BlockSpec((1, D_MODEL), lambda i: (0, 0))],
        out_specs=[pl.BlockSpec((tm, D_MODEL), row), pl.BlockSpec((tm, D_MODEL), row)],
        out_shape=[jax.ShapeDtypeStruct((n, D_MODEL), F32),
                   jax.ShapeDtypeStruct((n, D_MODEL), BF16)],
        compiler_params=_cparams("parallel"),
        name="out_proj",
    )(xin, w, h, ga, gb)


def _ffn_kernel(u_ref, wa_ref, wb_ref, cw_ref, cb_ref, wd_ref, h_ref, ga_ref, gb_ref,
                ho_ref, uo_ref, tail_ref, *, seq_tiles):
    i = pl.program_id(0)
    c = pl.program_id(1)
    tm = u_ref.shape[0]
    keep = tail_ref.shape[1]

    @pl.when(i % seq_tiles == 0)
    def _():
        tail_ref[c] = jnp.zeros(tail_ref.shape[1:], F32)

    @pl.when(c == 0)
    def _():
        ho_ref[...] = jnp.zeros_like(ho_ref)

    u = u_ref[...]
    a = _dot(u, wa_ref[...].astype(BF16))
    b = _dot(u, wb_ref[...].astype(BF16))
    prev = tail_ref[c]
    tail_ref[c] = a[tm - keep:tm]
    row = lax.broadcasted_iota(jnp.int32, a.shape, 0)
    last1 = prev[keep - 1:keep]
    last2 = prev[keep - 2:keep - 1]
    p1 = jnp.where(row == 0, last1, pltpu.roll(a, 1, 0))
    p2 = jnp.where(row == 0, last2, jnp.where(row == 1, last1, pltpu.roll(a, 2, 0)))
    cw = cw_ref[...]
    y = cw[0:1] * p2 + cw[1:2] * p1 + cw[2:3] * a + cb_ref[...]
    gated = y / (1.0 + jnp.exp(-y)) * b
    ho_ref[...] += _dot(gated.astype(BF16), wd_ref[...].astype(BF16))

    @pl.when(c == pl.num_programs(1) - 1)
    def _():
        _residual_update(ho_ref[...], h_ref, ga_ref, gb_ref, ho_ref, uo_ref)


def _conv_ffn(u, h, w_up, conv_w, conv_b, w_down, norm_g, layer, next_gain, seq):
    n = u.shape[0]
    tm, tf = FFN_ROW_TILE, FFN_COL_TILE
    n_chunks = FFN_DIM // tf
    row = lambda i, c: (i, 0)

    def row_spec(buffers=1):
        return pl.BlockSpec((tm, D_MODEL), row, pipeline_mode=pl.Buffered(buffers))

    def gain_spec(lyr, slot):
        return pl.BlockSpec((None, None, 1, D_MODEL), lambda i, c: (lyr, slot, 0, 0))

    return pl.pallas_call(
        functools.partial(_ffn_kernel, seq_tiles=seq // tm),
        grid=(n // tm, n_chunks),
        in_specs=[row_spec(),
                  pl.BlockSpec((None, D_MODEL, tf), lambda i, c: (layer, 0, c)),
                  pl.BlockSpec((None, D_MODEL, tf), lambda i, c: (layer, 0, c + n_chunks)),
                  pl.BlockSpec((None, CONV_WIDTH, tf), lambda i, c: (layer, 0, c)),
                  pl.BlockSpec((None, 1, tf), lambda i, c: (layer, 0, c)),
                  pl.BlockSpec((None, tf, D_MODEL), lambda i, c: (layer, c, 0)),
                  row_spec(),
                  gain_spec(layer, 3),
                  gain_spec(*next_gain)],
        out_specs=[row_spec(), row_spec(2)],
        out_shape=[jax.ShapeDtypeStruct((n, D_MODEL), F32),
                   jax.ShapeDtypeStruct((n, D_MODEL), BF16)],
        scratch_shapes=[pltpu.VMEM((n_chunks, SUBLANES, tf), F32)],
        compiler_params=pltpu.CompilerParams(dimension_semantics=("arbitrary", "arbitrary"),
                                             vmem_limit_bytes=FFN_VMEM_LIMIT),
        name="conv_ffn",
    )(u, w_up, w_up, conv_w, conv_b, w_down, h, norm_g, norm_g)


def _pool_kernel(u_ref, win_ref, wgrp_ref, scale_ref, wout_ref, h_ref, ga_ref, gb_ref,
                 ho_ref, uo_ref, tail_ref, m_ref, *, seq_tiles, norm_input):
    i = pl.program_id(0)
    tm = h_ref.shape[0]
    halo = tail_ref.shape[0]
    assert halo >= max(POOL_WINDOWS) - 1

    @pl.when(i % seq_tiles == 0)
    def _():
        tail_ref[...] = jnp.zeros_like(tail_ref)

    u = (_rms(h_ref[...]) * u_ref[...]).astype(BF16) if norm_input else u_ref[...]
    tpos = (i % seq_tiles) * tm + lax.broadcasted_iota(jnp.int32, (tm, 1), 0)
    for g, w in enumerate(POOL_WINDOWS):
        cols = slice(g * POOL_GROUP_DIM, (g + 1) * POOL_GROUP_DIM)
        z = _dot(u, win_ref[:, cols])
        x = jnp.concatenate([tail_ref[:, cols], z], axis=0)
        tail_ref[:, cols] = z[tm - halo:]
        s = x
        k = 1
        while k < w:
            s = s + pltpu.roll(s, k, 0)
            k *= 2
        cnt = jnp.minimum(tpos + 1, w).astype(F32)
        p = s[halo:] / cnt - z
        mg = _dot(p.astype(BF16), wgrp_ref[g]) * scale_ref[:, cols]
        m_ref[:, cols] = mg.astype(BF16)
    _residual_update(_dot(m_ref[...], wout_ref[...]), h_ref, ga_ref, gb_ref, ho_ref, uo_ref)


def _pool_mixer(u, h, w_in, w_grp, scale, w_out, ga, gb, layer, seq):
    n = h.shape[0]
    tm = ROW_TILE
    row = lambda i: (i, 0)
    norm_input = u.shape[0] == 1
    u_spec = _const_spec((1, D_MODEL)) if norm_input else pl.BlockSpec((tm, D_MODEL), row)
    return pl.pallas_call(
        functools.partial(_pool_kernel, seq_tiles=seq // tm, norm_input=norm_input),
        grid=(n // tm,),
        in_specs=[u_spec,
                  _layer_spec(w_in.shape, layer),
                  _layer_spec(w_grp.shape, layer),
                  _const_spec((1, D_MODEL)),
                  _layer_spec(w_out.shape, layer),
                  pl.BlockSpec((tm, D_MODEL), row),
                  _const_spec((1, D_MODEL)),
                  _const_spec((1, D_MODEL))],
        out_specs=[pl.BlockSpec((tm, D_MODEL), row), pl.BlockSpec((tm, D_MODEL), row)],
        out_shape=[jax.ShapeDtypeStruct((n, D_MODEL), F32),
                   jax.ShapeDtypeStruct((n, D_MODEL), BF16)],
        scratch_shapes=[pltpu.VMEM((BF16_SUBLANES, D_MODEL), F32), pltpu.VMEM((tm, D_MODEL), BF16)],
        compiler_params=_cparams("arbitrary"),
        name="pool_mixer",
    )(u, w_in, w_grp, scale, w_out, h, ga, gb)


def _gelu_tanh(x):
    c = math.sqrt(2.0 / math.pi)
    return 0.5 * x * (1.0 + jnp.tanh(c * (x + 0.044715 * (x * x * x))))


def _sgu_kernel(u_ref, win_ref, lng_ref, lnb_ref, ws_ref, bst_ref, wout_ref, h_ref, ga_ref, gb_ref,
                ho_ref, uo_ref, gated_ref):
    tm = u_ref.shape[0]
    y = _gelu_tanh(_dot(u_ref[...], win_ref[...]))
    uu = y[:, :D_MODEL]
    v = y[:, D_MODEL:]
    mu = jnp.mean(v, axis=-1, keepdims=True)
    vc = v - mu
    var = jnp.mean(vc * vc, axis=-1, keepdims=True)
    vn = (vc * lax.rsqrt(var + LN_EPS) * lng_ref[...] + lnb_ref[...]).astype(BF16)
    t_idx = lax.broadcasted_iota(jnp.int32, (SGU_CHUNK, SGU_CHUNK), 0)
    s_idx = lax.broadcasted_iota(jnp.int32, (SGU_CHUNK, SGU_CHUNK), 1)
    causal = s_idx <= t_idx
    bst = bst_ref[...]
    for g in range(SGU_GROUPS):
        cols = slice(g * SGU_GROUP_DIM, (g + 1) * SGU_GROUP_DIM)
        ws = jnp.where(causal, ws_ref[g], 0.0).astype(BF16)
        bias = bst[:, g:g + 1]
        for ci in range(tm // SGU_CHUNK):
            rows = slice(ci * SGU_CHUNK, (ci + 1) * SGU_CHUNK)
            mixed = _dot(ws, vn[rows, cols]) + bias
            gated_ref[rows, cols] = (uu[rows, cols] * mixed).astype(BF16)
    _residual_update(_dot(gated_ref[...], wout_ref[...]), h_ref, ga_ref, gb_ref, ho_ref, uo_ref)


def _sgu_mixer(u, h, w_in, ln_g, ln_b, w_s, b_s_t, w_out, ga, gb):
    n = u.shape[0]
    tm = MIX_ROW_TILE
    row = lambda i: (i, 0)
    return pl.pallas_call(
        _sgu_kernel,
        grid=(n // tm,),
        in_specs=[pl.BlockSpec((tm, D_MODEL), row),
                  _const_spec((D_MODEL, 2 * D_MODEL)),
                  _const_spec((1, D_MODEL)),
                  _const_spec((1, D_MODEL)),
                  _const_spec(w_s.shape),
                  _const_spec(b_s_t.shape),
                  _const_spec((D_MODEL, D_MODEL)),
                  pl.BlockSpec((tm, D_MODEL), row),
                  _const_spec((1, D_MODEL)),
                  _const_spec((1, D_MODEL))],
        out_specs=[pl.BlockSpec((tm, D_MODEL), row), pl.BlockSpec((tm, D_MODEL), row)],
        out_shape=[jax.ShapeDtypeStruct((n, D_MODEL), F32),
                   jax.ShapeDtypeStruct((n, D_MODEL), BF16)],
        scratch_shapes=[pltpu.VMEM((tm, D_MODEL), BF16)],
        compiler_params=_cparams("parallel"),
        name="sgu_mixer",
    )(u, w_in, ln_g, ln_b, w_s, b_s_t, w_out, h, ga, gb)


def _nsa_proj_kernel(u_ref, w_ref, bias_ref, oscale_ref, cos_ref, sin_ref, o_ref, *, rope_chunks,
                     gate_chunks):
    j = pl.program_id(1)
    acc = _dot(u_ref[...], w_ref[...])
    tn = acc.shape[1]

    def is_in(chunks):
        hit = j == chunks[0]
        for c in chunks[1:]:
            hit = hit | (j == c)
        return hit

    rope = is_in(rope_chunks)
    plain = jnp.logical_not(rope)
    if gate_chunks:
        gate = is_in(gate_chunks)
        plain = plain & jnp.logical_not(gate)

        @pl.when(gate)
        def _():
            o_ref[...] = (1.0 / (1.0 + jnp.exp(-(acc + bias_ref[...])))).astype(o_ref.dtype)

    @pl.when(rope)
    def _():
        cos = cos_ref[...]
        sin = sin_ref[...]
        for hh in range(tn // HEAD_DIM):
            cols = slice(hh * HEAD_DIM, (hh + 1) * HEAD_DIM)
            x = acc[:, cols]
            roped = x * cos + pltpu.roll(x, HEAD_DIM // 2, 1) * sin
            o_ref[:, cols] = (roped * oscale_ref[:, cols]).astype(o_ref.dtype)

    @pl.when(plain)
    def _():
        o_ref[...] = acc.astype(o_ref.dtype)


def _nsa_proj(u, w, bias, oscale, cos, sin, *, tn, out_dtype, rope_chunks, gate_chunks, seq):
    n = u.shape[0]
    ncols = w.shape[1]
    tm = PROJ_ROW_TILE
    seq_tiles = seq // tm
    return pl.pallas_call(
        functools.partial(_nsa_proj_kernel, rope_chunks=rope_chunks, gate_chunks=gate_chunks),
        grid=(n // tm, ncols // tn),
        in_specs=[pl.BlockSpec((tm, D_MODEL), lambda i, j: (i, 0)),
                  pl.BlockSpec((D_MODEL, tn), lambda i, j: (0, j)),
                  pl.BlockSpec((1, tn), lambda i, j: (0, j)),
                  pl.BlockSpec((1, tn), lambda i, j: (0, j)),
                  pl.BlockSpec((tm, HEAD_DIM), lambda i, j: (i % seq_tiles, 0)),
                  pl.BlockSpec((tm, HEAD_DIM), lambda i, j: (i % seq_tiles, 0))],
        out_specs=pl.BlockSpec((tm, tn), lambda i, j: (i, j)),
        out_shape=jax.ShapeDtypeStruct((n, ncols), out_dtype),
        compiler_params=_cparams("parallel", "arbitrary"),
        name="nsa_proj",
    )(u, w, bias, oscale, cos, sin)


def _nsa_vt_kernel(u_ref, w_ref, o_ref):
    acc = _dot(u_ref[...], w_ref[...])
    for kt in range(acc.shape[0] // ATT_TILE):
        for g in range(N_KV_HEADS):
            tile = acc[kt * ATT_TILE:(kt + 1) * ATT_TILE, g * HEAD_DIM:(g + 1) * HEAD_DIM]
            o_ref[kt, g] = tile.T.astype(o_ref.dtype)


def _nsa_vt(u, w):
    n = u.shape[0]
    tm = PROJ_ROW_TILE
    kt = tm // ATT_TILE
    return pl.pallas_call(
        _nsa_vt_kernel,
        grid=(n // tm, 2),
        in_specs=[pl.BlockSpec((tm, D_MODEL), lambda i, j: (i, 0)),
                  pl.BlockSpec((D_MODEL, NSA_KV_DIM), lambda i, j: (0, j))],
        out_specs=pl.BlockSpec((None, kt, N_KV_HEADS, HEAD_DIM, ATT_TILE), lambda i, j: (j, i, 0, 0, 0)),
        out_shape=jax.ShapeDtypeStruct((2, n // ATT_TILE, N_KV_HEADS, HEAD_DIM, ATT_TILE), BF16),
        compiler_params=_cparams("parallel", "arbitrary"),
        name="nsa_vt",
    )(u, w)


def _compress_kernel(a_ref, pe_ref, w1_ref, b1_ref, w2_ref, b2_ref, o_ref):
    nb = a_ref.shape[0] // CMP_STRIDE
    top = jnp.zeros((nb, CMP_HIDDEN), F32)
    bot = jnp.zeros((nb, CMP_HIDDEN), F32)
    for p in range(CMP_STRIDE):
        xp = a_ref[pl.ds(p, nb, stride=CMP_STRIDE), :]
        lo = slice(p * HEAD_DIM, (p + 1) * HEAD_DIM)
        hi = slice((CMP_STRIDE + p) * HEAD_DIM, (CMP_STRIDE + p + 1) * HEAD_DIM)
        top += _dot((xp + pe_ref[p:p + 1, :]).astype(BF16), w1_ref[lo, :].astype(BF16))
        bot += _dot((xp + pe_ref[CMP_STRIDE + p:CMP_STRIDE + p + 1, :]).astype(BF16),
                    w1_ref[hi, :].astype(BF16))
    hid = _gelu_tanh(top + pltpu.roll(bot, nb - 1, 0) + b1_ref[...])
    res = _dot(hid.astype(BF16), w2_ref[...].astype(BF16)) + b2_ref[...]

    @pl.when(pl.program_id(1) == 0)
    def _():
        o_ref[...] = res.astype(o_ref.dtype)

    @pl.when(pl.program_id(1) == 1)
    def _():
        o_ref[...] = res.T.astype(o_ref.dtype)


def _compress(cv, pe, w1, b1, w2, b2, batch, seq):
    nb = seq // CMP_STRIDE
    return pl.pallas_call(
        _compress_kernel,
        grid=(batch, 2, N_KV_HEADS),
        in_specs=[pl.BlockSpec((seq, HEAD_DIM), lambda b, s, g: (b, s * N_KV_HEADS + g)),
                  pl.BlockSpec((None, CMP_BLOCK, HEAD_DIM), lambda b, s, g: (s, 0, 0)),
                  pl.BlockSpec((None, CMP_BLOCK * HEAD_DIM, CMP_HIDDEN), lambda b, s, g: (s, 0, 0)),
                  pl.BlockSpec((None, 1, CMP_HIDDEN), lambda b, s, g: (s, 0, 0)),
                  pl.BlockSpec((None, CMP_HIDDEN, HEAD_DIM), lambda b, s, g: (s, 0, 0)),
                  pl.BlockSpec((None, 1, HEAD_DIM), lambda b, s, g: (s, 0, 0))],
        out_specs=pl.BlockSpec((None, None, None, nb, HEAD_DIM), lambda b, s, g: (b, s, g, 0, 0)),
        out_shape=jax.ShapeDtypeStruct((batch, 2, N_KV_HEADS, nb, HEAD_DIM), BF16),
        compiler_params=_cparams("parallel", "parallel", "parallel"),
        name="nsa_compress",
    )(cv, pe, w1, b1, w2, b2)


def _attn_update(carry, qt, branches, tq, slab):
    m, l, acc = carry
    ms, ls, accs = [], [], []
    for b, (k, vt, fix) in enumerate(branches):
        for r in range(GQA_REP):
            for lo in range(0, tq, slab):
                q0 = r * tq + lo
                c0 = b * GQA_REP * tq + q0
                z = fix(_dot(k, qt[:, q0:q0 + slab]), lo)
                m_old = m[:, c0:c0 + slab]
                m_new = jnp.maximum(m_old, jnp.max(z, axis=0, keepdims=True))
                alpha = jnp.exp2(m_old - m_new)
                p = jnp.exp2(z - m_new)
                ms.append(m_new)
                ls.append(alpha * l[:, c0:c0 + slab] + jnp.sum(p, axis=0, keepdims=True))
                accs.append(alpha * acc[:, c0:c0 + slab] + _dot(vt, p.astype(BF16)))
    return tuple(jnp.concatenate(x, axis=1) for x in (ms, ls, accs))


def _nsa_attn_kernel(q_ref, ks_ref, kw_ref, vst_ref, vwt_ref, kc_ref, vct_ref, gate_ref, ovt_ref,
                     o_ref, bias_ref, *, n_cmp):
    i = pl.program_id(2)
    tq = q_ref.shape[0]
    tk = ATT_K_STEP
    nq = GQA_REP * tq
    n_slc, n_win = ovt_ref.shape

    q = q_ref[...].astype(F32)
    qt = jnp.concatenate([q[:, r * HEAD_DIM:(r + 1) * HEAD_DIM].T for r in range(GQA_REP)],
                         axis=1).astype(BF16)
    rel = lax.broadcasted_iota(jnp.int32, (tk, tq), 1) - lax.broadcasted_iota(jnp.int32, (tk, tq), 0)

    win_idx = lax.broadcasted_iota(jnp.int32, (n_win, nq), 0)
    t_abs = i * tq + (lax.broadcasted_iota(jnp.int32, (n_win, nq), 1) & (tq - 1))
    sc = _dot(kc_ref[...], qt)
    ok_c = (win_idx * CMP_STRIDE + (CMP_BLOCK - 1) <= t_abs) & (win_idx < n_cmp)
    sc = jnp.where(ok_c, sc, NEG)
    e = jnp.exp2(sc - jnp.max(sc, axis=0, keepdims=True))
    p_c = e * (1.0 / jnp.sum(e, axis=0, keepdims=True))
    p_c = jnp.where(t_abs >= CMP_BLOCK - 1, p_c, 0.0)
    o_c = _dot(vct_ref[...], p_c.astype(BF16))

    p_sum = p_c[:, 0:tq]
    for r in range(1, GQA_REP):
        p_sum = p_sum + p_c[:, r * tq:(r + 1) * tq]
    p_hi = p_sum.astype(BF16)
    p_lo = (p_sum - p_hi.astype(F32)).astype(BF16)
    ovt = ovt_ref[...]
    imp = _dot(ovt, p_hi) + _dot(ovt, p_lo)
    blk = lax.broadcasted_iota(jnp.int32, (n_slc, tq), 0)
    cur = (i * tq + lax.broadcasted_iota(jnp.int32, (n_slc, tq), 1)) >> int(math.log2(SLC_BLOCK))
    forced = (blk == 0) | (blk == cur) | (blk == cur - 1)
    imp = jnp.where(forced, BIG, imp)
    imp = jnp.where(blk <= cur, imp, NEG)
    rank = jnp.zeros((n_slc, tq), jnp.int32)
    for k in range(n_slc):
        other = imp[k:k + 1, :]
        ahead = (other > imp) | ((other == imp) & (blk > k))
        rank = rank + ahead.astype(jnp.int32)
    bias_ref[...] = jnp.where((rank < SLC_TOPK) & (blk <= cur), 0.0, NEG)

    init = (jnp.full((1, nq), NEG, F32), jnp.zeros((1, nq), F32), jnp.zeros((HEAD_DIM, nq), F32))
    blocks_per_tile = tk // SLC_BLOCK
    tiles_per_step = tk // ATT_TILE
    diag0 = i * (tq // tk)
    band0 = jnp.maximum(diag0 - WIN // tk, 0)
    slab = LANES

    def keys(ref, kt):
        return ref[pl.ds(pl.multiple_of(kt * tk, tk), tk), :]

    def values_t(ref, kt):
        return jnp.concatenate([ref[tiles_per_step * kt + j] for j in range(tiles_per_step)], axis=1)

    def block_bias(kt):
        rows = [jnp.broadcast_to(bias_ref[pl.ds(blocks_per_tile * kt + j, 1), :], (SLC_BLOCK, tq))
                for j in range(blocks_per_tile)]
        return jnp.concatenate(rows, axis=0)

    def far_body(kt, c):
        bias = block_bias(kt)
        slc = (keys(ks_ref, kt), values_t(vst_ref, kt), lambda z, lo: z + bias[:, lo:lo + slab])
        return _attn_update(c, qt, [slc], tq, slab)

    def near_body(kt, c):
        bias = block_bias(kt)
        in_band = rel < WIN - (i * tq - kt * tk)
        slc = (keys(ks_ref, kt), values_t(vst_ref, kt), lambda z, lo: z + bias[:, lo:lo + slab])
        win = (keys(kw_ref, kt), values_t(vwt_ref, kt),
               lambda z, lo: jnp.where(in_band[:, lo:lo + slab], z, NEG))
        return _attn_update(c, qt, [slc, win], tq, slab)

    c_s = lax.fori_loop(0, band0, far_body, init)
    both = tuple(jnp.concatenate([s, w], axis=1) for s, w in zip(c_s, init))
    both = lax.fori_loop(band0, diag0, near_body, both)
    for j in range(tq // tk):
        kt = diag0 + j
        bias = block_bias(kt)
        causal = rel >= j * tk
        slc = (keys(ks_ref, kt), values_t(vst_ref, kt),
               lambda z, lo: jnp.where(causal[:, lo:lo + slab], z + bias[:, lo:lo + slab], NEG))
        win = (keys(kw_ref, kt), values_t(vwt_ref, kt),
               lambda z, lo: jnp.where(causal[:, lo:lo + slab], z, NEG))
        both = _attn_update(both, qt, [slc, win], tq, slab)
    _, l_sw, a_sw = both
    o_sw = a_sw * (1.0 / l_sw)
    o_s = o_sw[:, :nq]
    o_w = o_sw[:, nq:]

    gt = gate_ref[...].T

    def gate(k):
        return jnp.concatenate([gt[r * N_GATES + k:r * N_GATES + k + 1, :] for r in range(GQA_REP)],
                               axis=1)

    o_t = gate(0) * o_c + gate(1) * o_s + gate(2) * o_w
    for r in range(GQA_REP):
        o_ref[:, r * HEAD_DIM:(r + 1) * HEAD_DIM] = o_t[:, r * tq:(r + 1) * tq].T.astype(o_ref.dtype)


def _nsa_attention(qk, vt, cmp, cvg, ovt, batch, seq):
    n = qk.shape[0]
    tq = ATT_Q_TILE
    assert tq % ATT_K_STEP == 0 and tq <= WIN and WIN % ATT_K_STEP == 0
    assert ATT_K_STEP % ATT_TILE == 0 and ATT_TILE % SLC_BLOCK == 0
    qt = seq // tq
    gw = GQA_REP * HEAD_DIM
    k0 = NSA_Q_DIM // HEAD_DIM
    gate0 = 2 * N_KV_HEADS
    n_win = seq // CMP_STRIDE

    def k_spec(which):
        return pl.BlockSpec((seq, HEAD_DIM), lambda b, g, i: (b, k0 + which * N_KV_HEADS + g))

    def vt_spec(which):
        return pl.BlockSpec((None, seq // ATT_TILE, None, HEAD_DIM, ATT_TILE),
                            lambda b, g, i: (which, b, g, 0, 0))

    def cmp_spec(which):
        return pl.BlockSpec((None, None, None, n_win, HEAD_DIM), lambda b, g, i: (b, which, g, 0, 0))

    return pl.pallas_call(
        functools.partial(_nsa_attn_kernel, n_cmp=n_win - CMP_BLOCK // CMP_STRIDE + 1),
        grid=(batch, N_KV_HEADS, qt),
        in_specs=[pl.BlockSpec((tq, gw), lambda b, g, i: (b * qt + i, g)),
                  k_spec(0), k_spec(1), vt_spec(0), vt_spec(1),
                  cmp_spec(0), cmp_spec(1),
                  pl.BlockSpec((tq, LANES), lambda b, g, i: (b * qt + i, gate0 + g)),
                  pl.BlockSpec(ovt.shape, lambda b, g, i: (0, 0))],
        out_specs=pl.BlockSpec((tq, gw), lambda b, g, i: (b * qt + i, g)),
        out_shape=jax.ShapeDtypeStruct((n, NSA_Q_DIM), BF16),
        scratch_shapes=[pltpu.VMEM(ovt.shape[:1] + (tq,), F32)],
        compiler_params=_cparams("parallel", "parallel", "arbitrary"),
        name="nsa_attention",
    )(qk, qk, qk, vt, vt, cmp, cmp, cvg, ovt)


def _rope_tables(seq):
    half = HEAD_DIM // 2
    inv = 1.0 / (ROPE_THETA ** (jnp.arange(half, dtype=F32) / half))
    ang = jnp.arange(seq, dtype=F32)[:, None] * inv[None, :]
    cos = jnp.cos(ang)
    sin = jnp.sin(ang)
    return jnp.concatenate([cos, cos], axis=1), jnp.concatenate([-sin, sin], axis=1)


def _overlap_matrix(seq):
    n_win = seq // CMP_STRIDE
    n_cmp = n_win - CMP_BLOCK // CMP_STRIDE + 1
    sj = np.arange(seq // SLC_BLOCK)[:, None]
    ci = np.arange(n_win)[None, :]
    ov = ((ci * CMP_STRIDE <= (sj + 1) * SLC_BLOCK - 1)
          & (ci * CMP_STRIDE + CMP_BLOCK - 1 >= sj * SLC_BLOCK) & (ci < n_cmp))
    return jnp.asarray(ov, BF16)


def _nsa_mixer(u, h, w_in, gate_b, cmp_pe, cmp_w1, cmp_b1, cmp_w2, cmp_b2, w_out, ga, gb, batch, seq):
    assert seq // CMP_STRIDE == ATT_TILE and seq % PROJ_ROW_TILE == 0
    q0 = NSA_Q_DIM
    kvd = NSA_KV_DIM

    def part(k):
        return w_in[:, q0 + k * kvd:q0 + (k + 1) * kvd]

    w_qk = jnp.concatenate([w_in[:, :q0], part(2), part(4)], axis=1).astype(BF16)
    w_v = jnp.concatenate([part(3), part(5)], axis=1).astype(BF16)
    per_group = GQA_REP * N_GATES
    w_g = w_in[:, q0 + 6 * kvd:].reshape(D_MODEL, N_KV_HEADS, per_group)
    w_g = jnp.pad(w_g, ((0, 0), (0, 0), (0, LANES - per_group))).reshape(D_MODEL, N_KV_HEADS * LANES)
    b_g = jnp.pad(gate_b.reshape(N_KV_HEADS, per_group), ((0, 0), (0, LANES - per_group)))
    w_c = jnp.concatenate([part(0), part(1), w_g], axis=1).astype(BF16)
    bias_c = jnp.concatenate([jnp.zeros((2 * kvd,), F32), b_g.reshape(-1)])[None, :]
    cos, sin = _rope_tables(seq)

    tn = kvd
    q_scale = HEAD_DIM ** -0.5 * math.log2(math.e)
    oscale_qk = jnp.concatenate([jnp.full((q0,), q_scale, F32), jnp.ones((2 * kvd,), F32)])[None, :]
    qk = _nsa_proj(u, w_qk, jnp.zeros((1, w_qk.shape[1]), F32), oscale_qk, cos, sin, tn=tn, out_dtype=BF16,
                   rope_chunks=tuple(range(w_qk.shape[1] // tn)), gate_chunks=(), seq=seq)
    vt = _nsa_vt(u, w_v)
    cvg = _nsa_proj(u, w_c, bias_c, jnp.ones_like(bias_c), cos, sin, tn=tn, out_dtype=F32,
                    rope_chunks=(0,), gate_chunks=(2,), seq=seq)
    cmp = _compress(cvg, cmp_pe, cmp_w1, cmp_b1[:, None, :], cmp_w2, cmp_b2[:, None, :], batch, seq)
    o = _nsa_attention(qk, vt, cmp, cvg, _overlap_matrix(seq), batch, seq)
    return _out_proj(o, w_out.astype(BF16), h, ga, gb)


def kernel(x, norm_g, ffn_w_up, ffn_conv_w, ffn_conv_b, ffn_w_down, pool_w_in, pool_w_grp, pool_scale,
           pool_w_out, sgu_w_in, sgu_ln_g, sgu_ln_b, sgu_w_s, sgu_b_s, sgu_w_out, nsa_w_in, nsa_gate_b,
           nsa_cmp_pe, nsa_cmp_w1, nsa_cmp_b1, nsa_cmp_w2, nsa_cmp_b2, nsa_w_out):
    batch, seq, d = x.shape
    h = x.reshape(batch * seq, d)

    def gain(i, k):
        return norm_g[i, k][None, :]

    ffn_up, ffn_down = ffn_w_up, ffn_w_down
    pool_in, pool_grp, pool_out = pool_w_in.astype(BF16), pool_w_grp.astype(BF16), pool_w_out.astype(BF16)
    gains = norm_g[:, :, None, :]
    conv_b = ffn_conv_b[:, None, :]

    u = gain(0, 0)
    for i in range(DEPTH):
        kind, j = i % 3, i // 3
        ga, gb = gain(i, 1), gain(i, 2)
        if kind == 0:
            h, u = _pool_mixer(u, h, pool_in, pool_grp, pool_scale[j][None, :], pool_out, ga, gb, j, seq)
        elif kind == 1:
            h, u = _sgu_mixer(u, h, sgu_w_in[j].astype(BF16), sgu_ln_g[j][None, :], sgu_ln_b[j][None, :],
                              sgu_w_s[j], sgu_b_s[j].T, sgu_w_out[j].astype(BF16), ga, gb)
        else:
            h, u = _nsa_mixer(u, h, nsa_w_in[j], nsa_gate_b[j], nsa_cmp_pe[j], nsa_cmp_w1[j],
                              nsa_cmp_b1[j], nsa_cmp_w2[j], nsa_cmp_b2[j], nsa_w_out[j], ga, gb,
                              batch, seq)
        next_gain = (i + 1, 0) if i + 1 < DEPTH else (i, 3)
        h, u = _conv_ffn(u, h, ffn_up, ffn_conv_w, conv_b, ffn_down, gains, i, next_gain, seq)
    return h.reshape(batch, seq, d)
```

```python
import functools
import math

import jax
import jax.numpy as jnp
import numpy as np
from jax import lax
from jax.experimental import pallas as pl
from jax.experimental.pallas import tpu as pltpu

F32 = jnp.float32
BF16 = jnp.bfloat16

D_MODEL = 2048
DEPTH = 4
RMS_EPS = 1e-6
LN_EPS = 1e-5
NEG = -1e30
BIG = 1e30

FFN_DIM = 5632
CONV_WIDTH = 3
POOL_WINDOWS = (2, 4, 8, 16)
POOL_GROUP_DIM = D_MODEL // len(POOL_WINDOWS)
SGU_CHUNK = 128
SGU_GROUPS = 16
SGU_GROUP_DIM = D_MODEL // SGU_GROUPS

HEAD_DIM = 128
N_HEADS = 16
N_KV_HEADS = 4
GQA_REP = N_HEADS // N_KV_HEADS
ROPE_THETA = 10000.0
CMP_BLOCK = 32
CMP_STRIDE = 16
CMP_HIDDEN = 2 * HEAD_DIM
SLC_BLOCK = 64
SLC_TOPK = 16
WIN = 512
NSA_Q_DIM = N_HEADS * HEAD_DIM
NSA_KV_DIM = N_KV_HEADS * HEAD_DIM
N_GATES = 3

LANES = 128
SUBLANES = 8
BF16_SUBLANES = 16
VMEM_LIMIT = 56 * 1024 * 1024
FFN_VMEM_LIMIT = 60 * 1024 * 1024

ROW_TILE = 512
PROJ_ROW_TILE = 1024
MIX_ROW_TILE = 256
FFN_ROW_TILE = 1024
FFN_COL_TILE = 512
ATT_TILE = 128
ATT_K_STEP = 256
ATT_Q_TILE = 512


def _cparams(*sem):
    return pltpu.CompilerParams(dimension_semantics=sem, vmem_limit_bytes=VMEM_LIMIT)


def _const_spec(shape):
    nd = len(shape)
    return pl.BlockSpec(shape, lambda *_: (0,) * nd, pipeline_mode=pl.Buffered(1))


def _layer_spec(stacked_shape, layer):
    nd = len(stacked_shape) - 1
    return pl.BlockSpec((None,) + tuple(stacked_shape[1:]), lambda *_: (layer,) + (0,) * nd,
                        pipeline_mode=pl.Buffered(1))


def _rms(x):
    return x * lax.rsqrt(jnp.mean(x * x, axis=-1, keepdims=True) + RMS_EPS)


def _residual_update(m, h_ref, ga_ref, gb_ref, ho_ref, uo_ref):
    hn = h_ref[...] + _rms(m) * ga_ref[...]
    ho_ref[...] = hn
    uo_ref[...] = (_rms(hn) * gb_ref[...]).astype(BF16)


def _dot(a, b):
    return jnp.dot(a, b, preferred_element_type=F32)


def _out_proj_kernel(x_ref, w_ref, h_ref, ga_ref, gb_ref, ho_ref, uo_ref):
    _residual_update(_dot(x_ref[...], w_ref[...]), h_ref, ga_ref, gb_ref, ho_ref, uo_ref)


def _out_proj(xin, w, h, ga, gb):
    n, k = xin.shape
    tm = ROW_TILE
    row = lambda i: (i, 0)
    return pl.pallas_call(
        _out_proj_kernel,
        grid=(n // tm,),
        in_specs=[pl.BlockSpec((tm, k), row), _const_spec((k, D_MODEL)),
                  pl.BlockSpec((tm, D_MODEL), row),
                  pl.BlockSpec((1, D_MODEL), lambda i: (0, 0)),
                  pl.BlockSpec((1, D_MODEL), lambda i: (0, 0))],
        out_specs=[pl.BlockSpec((tm, D_MODEL), row), pl.BlockSpec((tm, D_MODEL), row)],
        out_shape=[jax.ShapeDtypeStruct((n, D_MODEL), F32),
                   jax.ShapeDtypeStruct((n, D_MODEL), BF16)],
        compiler_params=_cparams("parallel"),
        name="out_proj",
    )(xin, w, h, ga, gb)


def _ffn_kernel(u_ref, wa_ref, wb_ref, cw_ref, cb_ref, wd_ref, h_hbm, ga_ref, gb_ref,
                ho_hbm, uo_ref, tail_ref, acc_ref, hbuf_ref, sem_ref, *, seq_tiles):
    i = pl.program_id(0)
    c = pl.program_id(1)
    n_i = pl.num_programs(0)
    n_c = pl.num_programs(1)
    tm = u_ref.shape[0]
    keep = tail_ref.shape[1]

    def rows(tile):
        return pl.ds(pl.multiple_of(tile * tm, tm), tm)

    def h_load():
        return pltpu.make_async_copy(h_hbm.at[rows(i)], hbuf_ref, sem_ref.at[0])

    def h_store(tile):
        return pltpu.make_async_copy(hbuf_ref, ho_hbm.at[rows(tile)], sem_ref.at[1])

    @pl.when(c == 1)
    def _():
        @pl.when(i > 0)
        def _():
            h_store(i - 1).wait()
        h_load().start()

    @pl.when(i % seq_tiles == 0)
    def _():
        tail_ref[c] = jnp.zeros(tail_ref.shape[1:], F32)

    @pl.when(c == 0)
    def _():
        acc_ref[...] = jnp.zeros_like(acc_ref)

    u = u_ref[...]
    a = _dot(u, wa_ref[...].astype(BF16))
    b = _dot(u, wb_ref[...].astype(BF16))
    prev = tail_ref[c]
    tail_ref[c] = a[tm - keep:tm]
    row = lax.broadcasted_iota(jnp.int32, a.shape, 0)
    last1 = prev[keep - 1:keep]
    last2 = prev[keep - 2:keep - 1]
    p1 = jnp.where(row == 0, last1, pltpu.roll(a, 1, 0))
    p2 = jnp.where(row == 0, last2, jnp.where(row == 1, last1, pltpu.roll(a, 2, 0)))
    cw = cw_ref[...]
    y = cw[0:1] * p2 + cw[1:2] * p1 + cw[2:3] * a + cb_ref[...]
    gated = y / (1.0 + jnp.exp(-y)) * b
    acc_ref[...] += _dot(gated.astype(BF16), wd_ref[...].astype(BF16))

    @pl.when(c == n_c - 1)
    def _():
        h_load().wait()
        _residual_update(acc_ref[...], hbuf_ref, ga_ref, gb_ref, hbuf_ref, uo_ref)
        h_store(i).start()

        @pl.when(i == n_i - 1)
        def _():
            h_store(i).wait()


def _conv_ffn(u, h, w_up, conv_w, conv_b, w_down, norm_g, layer, next_gain, seq):
    n = u.shape[0]
    tm, tf = FFN_ROW_TILE, FFN_COL_TILE
    n_chunks = FFN_DIM // tf
    row = lambda i, c: (i, 0)

    def row_spec(buffers=1):
        return pl.BlockSpec((tm, D_MODEL), row, pipeline_mode=pl.Buffered(buffers))

    def gain_spec(lyr, slot):
        return pl.BlockSpec((None, None, 1, D_MODEL), lambda i, c: (lyr, slot, 0, 0))

    return pl.pallas_call(
        functools.partial(_ffn_kernel, seq_tiles=seq // tm),
        grid=(n // tm, n_chunks),
        in_specs=[row_spec(),
                  pl.BlockSpec((None, D_MODEL, tf), lambda i, c: (layer, 0, c)),
                  pl.BlockSpec((None, D_MODEL, tf), lambda i, c: (layer, 0, c + n_chunks)),
                  pl.BlockSpec((None, CONV_WIDTH, tf), lambda i, c: (layer, 0, c)),
                  pl.BlockSpec((None, 1, tf), lambda i, c: (layer, 0, c)),
                  pl.BlockSpec((None, tf, D_MODEL), lambda i, c: (layer, c, 0)),
                  pl.BlockSpec(memory_space=pl.ANY),
                  gain_spec(layer, 3),
                  gain_spec(*next_gain)],
        out_specs=[pl.BlockSpec(memory_space=pl.ANY), row_spec(2)],
        out_shape=[jax.ShapeDtypeStruct((n, D_MODEL), F32),
                   jax.ShapeDtypeStruct((n, D_MODEL), BF16)],
        scratch_shapes=[pltpu.VMEM((n_chunks, SUBLANES, tf), F32),
                        pltpu.VMEM((tm, D_MODEL), F32),
                        pltpu.VMEM((tm, D_MODEL), F32),
                        pltpu.SemaphoreType.DMA((2,))],
        compiler_params=pltpu.CompilerParams(dimension_semantics=("arbitrary", "arbitrary"),
                                             vmem_limit_bytes=FFN_VMEM_LIMIT),
        name="conv_ffn",
    )(u, w_up, w_up, conv_w, conv_b, w_down, h, norm_g, norm_g)


def _pool_kernel(u_ref, win_ref, wgrp_ref, scale_ref, wout_ref, h_ref, ga_ref, gb_ref,
                 ho_ref, uo_ref, tail_ref, m_ref, *, seq_tiles, norm_input):
    i = pl.program_id(0)
    tm = h_ref.shape[0]
    halo = tail_ref.shape[0]
    assert halo >= max(POOL_WINDOWS) - 1

    @pl.when(i % seq_tiles == 0)
    def _():
        tail_ref[...] = jnp.zeros_like(tail_ref)

    u = (_rms(h_ref[...]) * u_ref[...]).astype(BF16) if norm_input else u_ref[...]
    tpos = (i % seq_tiles) * tm + lax.broadcasted_iota(jnp.int32, (tm, 1), 0)
    for g, w in enumerate(POOL_WINDOWS):
        cols = slice(g * POOL_GROUP_DIM, (g + 1) * POOL_GROUP_DIM)
        z = _dot(u, win_ref[:, cols])
        x = jnp.concatenate([tail_ref[:, cols], z], axis=0)
        tail_ref[:, cols] = z[tm - halo:]
        s = x
        k = 1
        while k < w:
            s = s + pltpu.roll(s, k, 0)
            k *= 2
        cnt = jnp.minimum(tpos + 1, w).astype(F32)
        p = s[halo:] / cnt - z
        mg = _dot(p.astype(BF16), wgrp_ref[g]) * scale_ref[:, cols]
        m_ref[:, cols] = mg.astype(BF16)
    _residual_update(_dot(m_ref[...], wout_ref[...]), h_ref, ga_ref, gb_ref, ho_ref, uo_ref)


def _pool_mixer(u, h, w_in, w_grp, scale, w_out, ga, gb, layer, seq):
    n = h.shape[0]
    tm = ROW_TILE
    row = lambda i: (i, 0)
    norm_input = u.shape[0] == 1
    u_spec = _const_spec((1, D_MODEL)) if norm_input else pl.BlockSpec((tm, D_MODEL), row)
    return pl.pallas_call(
        functools.partial(_pool_kernel, seq_tiles=seq // tm, norm_input=norm_input),
        grid=(n // tm,),
        in_specs=[u_spec,
                  _layer_spec(w_in.shape, layer),
                  _layer_spec(w_grp.shape, layer),
                  _const_spec((1, D_MODEL)),
                  _layer_spec(w_out.shape, layer),
                  pl.BlockSpec((tm, D_MODEL), row),
                  _const_spec((1, D_MODEL)),
                  _const_spec((1, D_MODEL))],
        out_specs=[pl.BlockSpec((tm, D_MODEL), row), pl.BlockSpec((tm, D_MODEL), row)],
        out_shape=[jax.ShapeDtypeStruct((n, D_MODEL), F32),
                   jax.ShapeDtypeStruct((n, D_MODEL), BF16)],
        scratch_shapes=[pltpu.VMEM((BF16_SUBLANES, D_MODEL), F32), pltpu.VMEM((tm, D_MODEL), BF16)],
        compiler_params=_cparams("arbitrary"),
        name="pool_mixer",
    )(u, w_in, w_grp, scale, w_out, h, ga, gb)


def _gelu_tanh(x):
    c = math.sqrt(2.0 / math.pi)
    return 0.5 * x * (1.0 + jnp.tanh(c * (x + 0.044715 * (x * x * x))))


def _sgu_kernel(u_ref, win_ref, lng_ref, lnb_ref, ws_ref, bst_ref, wout_ref, h_ref, ga_ref, gb_ref,
                ho_ref, uo_ref, gated_ref):
    tm = u_ref.shape[0]
    y = _gelu_tanh(_dot(u_ref[...], win_ref[...]))
    uu = y[:, :D_MODEL]
    v = y[:, D_MODEL:]
    mu = jnp.mean(v, axis=-1, keepdims=True)
    vc = v - mu
    var = jnp.mean(vc * vc, axis=-1, keepdims=True)
    vn = (vc * lax.rsqrt(var + LN_EPS) * lng_ref[...] + lnb_ref[...]).astype(BF16)
    t_idx = lax.broadcasted_iota(jnp.int32, (SGU_CHUNK, SGU_CHUNK), 0)
    s_idx = lax.broadcasted_iota(jnp.int32, (SGU_CHUNK, SGU_CHUNK), 1)
    causal = s_idx <= t_idx
    bst = bst_ref[...]
    for g in range(SGU_GROUPS):
        cols = slice(g * SGU_GROUP_DIM, (g + 1) * SGU_GROUP_DIM)
        ws = jnp.where(causal, ws_ref[g], 0.0).astype(BF16)
        bias = bst[:, g:g + 1]
        for ci in range(tm // SGU_CHUNK):
            rows = slice(ci * SGU_CHUNK, (ci + 1) * SGU_CHUNK)
            mixed = _dot(ws, vn[rows, cols]) + bias
            gated_ref[rows, cols] = (uu[rows, cols] * mixed).astype(BF16)
    _residual_update(_dot(gated_ref[...], wout_ref[...]), h_ref, ga_ref, gb_ref, ho_ref, uo_ref)


def _sgu_mixer(u, h, w_in, ln_g, ln_b, w_s, b_s_t, w_out, ga, gb):
    n = u.shape[0]
    tm = MIX_ROW_TILE
    row = lambda i: (i, 0)
    return pl.pallas_call(
        _sgu_kernel,
        grid=(n // tm,),
        in_specs=[pl.BlockSpec((tm, D_MODEL), row),
                  _const_spec((D_MODEL, 2 * D_MODEL)),
                  _const_spec((1, D_MODEL)),
                  _const_spec((1, D_MODEL)),
                  _const_spec(w_s.shape),
                  _const_spec(b_s_t.shape),
                  _const_spec((D_MODEL, D_MODEL)),
                  pl.BlockSpec((tm, D_MODEL), row),
                  _const_spec((1, D_MODEL)),
                  _const_spec((1, D_MODEL))],
        out_specs=[pl.BlockSpec((tm, D_MODEL), row), pl.BlockSpec((tm, D_MODEL), row)],
        out_shape=[jax.ShapeDtypeStruct((n, D_MODEL), F32),
                   jax.ShapeDtypeStruct((n, D_MODEL), BF16)],
        scratch_shapes=[pltpu.VMEM((tm, D_MODEL), BF16)],
        compiler_params=_cparams("parallel"),
        name="sgu_mixer",
    )(u, w_in, ln_g, ln_b, w_s, b_s_t, w_out, h, ga, gb)


def _nsa_proj_kernel(u_ref, w_ref, bias_ref, oscale_ref, cos_ref, sin_ref, o_ref, *, rope_chunks,
                     gate_chunks):
    j = pl.program_id(1)
    acc = _dot(u_ref[...], w_ref[...])
    tn = acc.shape[1]

    def is_in(chunks):
        hit = j == chunks[0]
        for c in chunks[1:]:
            hit = hit | (j == c)
        return hit

    rope = is_in(rope_chunks)
    plain = jnp.logical_not(rope)
    if gate_chunks:
        gate = is_in(gate_chunks)
        plain = plain & jnp.logical_not(gate)

        @pl.when(gate)
        def _():
            o_ref[...] = (1.0 / (1.0 + jnp.exp(-(acc + bias_ref[...])))).astype(o_ref.dtype)

    @pl.when(rope)
    def _():
        cos = cos_ref[...]
        sin = sin_ref[...]
        for hh in range(tn // HEAD_DIM):
            cols = slice(hh * HEAD_DIM, (hh + 1) * HEAD_DIM)
            x = acc[:, cols]
            roped = x * cos + pltpu.roll(x, HEAD_DIM // 2, 1) * sin
            o_ref[:, cols] = (roped * oscale_ref[:, cols]).astype(o_ref.dtype)

    @pl.when(plain)
    def _():
        o_ref[...] = acc.astype(o_ref.dtype)


def _nsa_proj(u, w, bias, oscale, cos, sin, *, tn, out_dtype, rope_chunks, gate_chunks, seq):
    n = u.shape[0]
    ncols = w.shape[1]
    tm = PROJ_ROW_TILE
    seq_tiles = seq // tm
    return pl.pallas_call(
        functools.partial(_nsa_proj_kernel, rope_chunks=rope_chunks, gate_chunks=gate_chunks),
        grid=(n // tm, ncols // tn),
        in_specs=[pl.BlockSpec((tm, D_MODEL), lambda i, j: (i, 0)),
                  pl.BlockSpec((D_MODEL, tn), lambda i, j: (0, j)),
                  pl.BlockSpec((1, tn), lambda i, j: (0, j)),
                  pl.BlockSpec((1, tn), lambda i, j: (0, j)),
                  pl.BlockSpec((tm, HEAD_DIM), lambda i, j: (i % seq_tiles, 0)),
                  pl.BlockSpec((tm, HEAD_DIM), lambda i, j: (i % seq_tiles, 0))],
        out_specs=pl.BlockSpec((tm, tn), lambda i, j: (i, j)),
        out_shape=jax.ShapeDtypeStruct((n, ncols), out_dtype),
        compiler_params=_cparams("parallel", "arbitrary"),
        name="nsa_proj",
    )(u, w, bias, oscale, cos, sin)


def _nsa_vt_kernel(u_ref, w_ref, o_ref):
    acc = _dot(u_ref[...], w_ref[...])
    for kt in range(acc.shape[0] // ATT_TILE):
        for g in range(N_KV_HEADS):
            tile = acc[kt * ATT_TILE:(kt + 1) * ATT_TILE, g * HEAD_DIM:(g + 1) * HEAD_DIM]
            o_ref[kt, g] = tile.T.astype(o_ref.dtype)


def _nsa_vt(u, w):
    n = u.shape[0]
    tm = PROJ_ROW_TILE
    kt = tm // ATT_TILE
    return pl.pallas_call(
        _nsa_vt_kernel,
        grid=(n // tm, 2),
        in_specs=[pl.BlockSpec((tm, D_MODEL), lambda i, j: (i, 0)),
                  pl.BlockSpec((D_MODEL, NSA_KV_DIM), lambda i, j: (0, j))],
        out_specs=pl.BlockSpec((None, kt, N_KV_HEADS, HEAD_DIM, ATT_TILE), lambda i, j: (j, i, 0, 0, 0)),
        out_shape=jax.ShapeDtypeStruct((2, n // ATT_TILE, N_KV_HEADS, HEAD_DIM, ATT_TILE), BF16),
        compiler_params=_cparams("parallel", "arbitrary"),
        name="nsa_vt",
    )(u, w)


def _compress_kernel(a_ref, pe_ref, w1_ref, b1_ref, w2_ref, b2_ref, o_ref):
    nb = a_ref.shape[0] // CMP_STRIDE
    top = jnp.zeros((nb, CMP_HIDDEN), F32)
    bot = jnp.zeros((nb, CMP_HIDDEN), F32)
    for p in range(CMP_STRIDE):
        xp = a_ref[pl.ds(p, nb, stride=CMP_STRIDE), :]
        lo = slice(p * HEAD_DIM, (p + 1) * HEAD_DIM)
        hi = slice((CMP_STRIDE + p) * HEAD_DIM, (CMP_STRIDE + p + 1) * HEAD_DIM)
        top += _dot((xp + pe_ref[p:p + 1, :]).astype(BF16), w1_ref[lo, :].astype(BF16))
        bot += _dot((xp + pe_ref[CMP_STRIDE + p:CMP_STRIDE + p + 1, :]).astype(BF16),
                    w1_ref[hi, :].astype(BF16))
    hid = _gelu_tanh(top + pltpu.roll(bot, nb - 1, 0) + b1_ref[...])
    res = _dot(hid.astype(BF16), w2_ref[...].astype(BF16)) + b2_ref[...]

    @pl.when(pl.program_id(1) == 0)
    def _():
        o_ref[...] = res.astype(o_ref.dtype)

    @pl.when(pl.program_id(1) == 1)
    def _():
        o_ref[...] = res.T.astype(o_ref.dtype)


def _compress(cv, pe, w1, b1, w2, b2, batch, seq):
    nb = seq // CMP_STRIDE
    return pl.pallas_call(
        _compress_kernel,
        grid=(batch, 2, N_KV_HEADS),
        in_specs=[pl.BlockSpec((seq, HEAD_DIM), lambda b, s, g: (b, s * N_KV_HEADS + g)),
                  pl.BlockSpec((None, CMP_BLOCK, HEAD_DIM), lambda b, s, g: (s, 0, 0)),
                  pl.BlockSpec((None, CMP_BLOCK * HEAD_DIM, CMP_HIDDEN), lambda b, s, g: (s, 0, 0)),
                  pl.BlockSpec((None, 1, CMP_HIDDEN), lambda b, s, g: (s, 0, 0)),
                  pl.BlockSpec((None, CMP_HIDDEN, HEAD_DIM), lambda b, s, g: (s, 0, 0)),
                  pl.BlockSpec((None, 1, HEAD_DIM), lambda b, s, g: (s, 0, 0))],
        out_specs=pl.BlockSpec((None, None, None, nb, HEAD_DIM), lambda b, s, g: (b, s, g, 0, 0)),
        out_shape=jax.ShapeDtypeStruct((batch, 2, N_KV_HEADS, nb, HEAD_DIM), BF16),
        compiler_params=_cparams("parallel", "parallel", "parallel"),
        name="nsa_compress",
    )(cv, pe, w1, b1, w2, b2)


def _attn_update(carry, qt, branches, tq, slab):
    m, l, acc = carry
    ms, ls, accs = [], [], []
    for b, (k, vt, fix) in enumerate(branches):
        for r in range(GQA_REP):
            for lo in range(0, tq, slab):
                q0 = r * tq + lo
                c0 = b * GQA_REP * tq + q0
                z = fix(_dot(k, qt[:, q0:q0 + slab]), lo)
                m_old = m[:, c0:c0 + slab]
                m_new = jnp.maximum(m_old, jnp.max(z, axis=0, keepdims=True))
                alpha = jnp.exp2(m_old - m_new)
                p = jnp.exp2(z - m_new)
                ms.append(m_new)
                ls.append(alpha * l[:, c0:c0 + slab] + jnp.sum(p, axis=0, keepdims=True))
                accs.append(alpha * acc[:, c0:c0 + slab] + _dot(vt, p.astype(BF16)))
    return tuple(jnp.concatenate(x, axis=1) for x in (ms, ls, accs))


def _nsa_attn_kernel(q_ref, ks_ref, kw_ref, vst_ref, vwt_ref, kc_ref, vct_ref, gate_ref, ovt_ref,
                     o_ref, bias_ref, *, n_cmp):
    i = pl.program_id(2)
    tq = q_ref.shape[0]
    tk = ATT_K_STEP
    nq = GQA_REP * tq
    n_slc, n_win = ovt_ref.shape

    q = q_ref[...].astype(F32)
    qt = jnp.concatenate([q[:, r * HEAD_DIM:(r + 1) * HEAD_DIM].T for r in range(GQA_REP)],
                         axis=1).astype(BF16)
    rel = lax.broadcasted_iota(jnp.int32, (tk, tq), 1) - lax.broadcasted_iota(jnp.int32, (tk, tq), 0)

    win_idx = lax.broadcasted_iota(jnp.int32, (n_win, nq), 0)
    t_abs = i * tq + (lax.broadcasted_iota(jnp.int32, (n_win, nq), 1) & (tq - 1))
    sc = _dot(kc_ref[...], qt)
    ok_c = (win_idx * CMP_STRIDE + (CMP_BLOCK - 1) <= t_abs) & (win_idx < n_cmp)
    sc = jnp.where(ok_c, sc, NEG)
    e = jnp.exp2(sc - jnp.max(sc, axis=0, keepdims=True))
    p_c = e * (1.0 / jnp.sum(e, axis=0, keepdims=True))
    p_c = jnp.where(t_abs >= CMP_BLOCK - 1, p_c, 0.0)
    o_c = _dot(vct_ref[...], p_c.astype(BF16))

    p_sum = p_c[:, 0:tq]
    for r in range(1, GQA_REP):
        p_sum = p_sum + p_c[:, r * tq:(r + 1) * tq]
    p_hi = p_sum.astype(BF16)
    p_lo = (p_sum - p_hi.astype(F32)).astype(BF16)
    ovt = ovt_ref[...]
    imp = _dot(ovt, p_hi) + _dot(ovt, p_lo)
    blk = lax.broadcasted_iota(jnp.int32, (n_slc, tq), 0)
    cur = (i * tq + lax.broadcasted_iota(jnp.int32, (n_slc, tq), 1)) >> int(math.log2(SLC_BLOCK))
    forced = (blk == 0) | (blk == cur) | (blk == cur - 1)
    imp = jnp.where(forced, BIG, imp)
    imp = jnp.where(blk <= cur, imp, NEG)
    rank = jnp.zeros((n_slc, tq), jnp.int32)
    for k in range(n_slc):
        other = imp[k:k + 1, :]
        ahead = (other > imp) | ((other == imp) & (blk > k))
        rank = rank + ahead.astype(jnp.int32)
    bias_ref[...] = jnp.where((rank < SLC_TOPK) & (blk <= cur), 0.0, NEG)

    init = (jnp.full((1, nq), NEG, F32), jnp.zeros((1, nq), F32), jnp.zeros((HEAD_DIM, nq), F32))
    blocks_per_tile = tk // SLC_BLOCK
    tiles_per_step = tk // ATT_TILE
    diag0 = i * (tq // tk)
    band0 = jnp.maximum(diag0 - WIN // tk, 0)
    slab = LANES

    def keys(ref, kt):
        return ref[pl.ds(pl.multiple_of(kt * tk, tk), tk), :]

    def values_t(ref, kt):
        return jnp.concatenate([ref[tiles_per_step * kt + j] for j in range(tiles_per_step)], axis=1)

    def block_bias(kt):
        rows = [jnp.broadcast_to(bias_ref[pl.ds(blocks_per_tile * kt + j, 1), :], (SLC_BLOCK, tq))
                for j in range(blocks_per_tile)]
        return jnp.concatenate(rows, axis=0)

    def far_body(kt, c):
        bias = block_bias(kt)
        slc = (keys(ks_ref, kt), values_t(vst_ref, kt), lambda z, lo: z + bias[:, lo:lo + slab])
        return _attn_update(c, qt, [slc], tq, slab)

    def near_body(kt, c):
        bias = block_bias(kt)
        in_band = rel < WIN - (i * tq - kt * tk)
        slc = (keys(ks_ref, kt), values_t(vst_ref, kt), lambda z, lo: z + bias[:, lo:lo + slab])
        win = (keys(kw_ref, kt), values_t(vwt_ref, kt),
               lambda z, lo: jnp.where(in_band[:, lo:lo + slab], z, NEG))
        return _attn_update(c, qt, [slc, win], tq, slab)

    c_s = lax.fori_loop(0, band0, far_body, init)
    both = tuple(jnp.concatenate([s, w], axis=1) for s, w in zip(c_s, init))
    both = lax.fori_loop(band0, diag0, near_body, both)
    for j in range(tq // tk):
        kt = diag0 + j
        bias = block_bias(kt)
        causal = rel >= j * tk
        slc = (keys(ks_ref, kt), values_t(vst_ref, kt),
               lambda z, lo: jnp.where(causal[:, lo:lo + slab], z + bias[:, lo:lo + slab], NEG))
        win = (keys(kw_ref, kt), values_t(vwt_ref, kt),
               lambda z, lo: jnp.where(causal[:, lo:lo + slab], z, NEG))
        both = _attn_update(both, qt, [slc, win], tq, slab)
    _, l_sw, a_sw = both
    o_sw = a_sw * (1.0 / l_sw)
    o_s = o_sw[:, :nq]
    o_w = o_sw[:, nq:]

    gt = gate_ref[...].T

    def gate(k):
        return jnp.concatenate([gt[r * N_GATES + k:r * N_GATES + k + 1, :] for r in range(GQA_REP)],
                               axis=1)

    o_t = gate(0) * o_c + gate(1) * o_s + gate(2) * o_w
    for r in range(GQA_REP):
        o_ref[:, r * HEAD_DIM:(r + 1) * HEAD_DIM] = o_t[:, r * tq:(r + 1) * tq].T.astype(o_ref.dtype)


def _nsa_attention(qk, vt, cmp, cvg, ovt, batch, seq):
    n = qk.shape[0]
    tq = ATT_Q_TILE
    assert tq % ATT_K_STEP == 0 and tq <= WIN and WIN % ATT_K_STEP == 0
    assert ATT_K_STEP % ATT_TILE == 0 and ATT_TILE % SLC_BLOCK == 0
    qt = seq // tq
    gw = GQA_REP * HEAD_DIM
    k0 = NSA_Q_DIM // HEAD_DIM
    gate0 = 2 * N_KV_HEADS
    n_win = seq // CMP_STRIDE

    def k_spec(which):
        return pl.BlockSpec((seq, HEAD_DIM), lambda b, g, i: (b, k0 + which * N_KV_HEADS + g))

    def vt_spec(which):
        return pl.BlockSpec((None, seq // ATT_TILE, None, HEAD_DIM, ATT_TILE),
                            lambda b, g, i: (which, b, g, 0, 0))

    def cmp_spec(which):
        return pl.BlockSpec((None, None, None, n_win, HEAD_DIM), lambda b, g, i: (b, which, g, 0, 0))

    return pl.pallas_call(
        functools.partial(_nsa_attn_kernel, n_cmp=n_win - CMP_BLOCK // CMP_STRIDE + 1),
        grid=(batch, N_KV_HEADS, qt),
        in_specs=[pl.BlockSpec((tq, gw), lambda b, g, i: (b * qt + i, g)),
                  k_spec(0), k_spec(1), vt_spec(0), vt_spec(1),
                  cmp_spec(0), cmp_spec(1),
                  pl.BlockSpec((tq, LANES), lambda b, g, i: (b * qt + i, gate0 + g)),
                  pl.BlockSpec(ovt.shape, lambda b, g, i: (0, 0))],
        out_specs=pl.BlockSpec((tq, gw), lambda b, g, i: (b * qt + i, g)),
        out_shape=jax.ShapeDtypeStruct((n, NSA_Q_DIM), BF16),
        scratch_shapes=[pltpu.VMEM(ovt.shape[:1] + (tq,), F32)],
        compiler_params=_cparams("parallel", "parallel", "arbitrary"),
        name="nsa_attention",
    )(qk, qk, qk, vt, vt, cmp, cmp, cvg, ovt)


def _rope_tables(seq):
    half = HEAD_DIM // 2
    inv = 1.0 / (ROPE_THETA ** (jnp.arange(half, dtype=F32) / half))
    ang = jnp.arange(seq, dtype=F32)[:, None] * inv[None, :]
    cos = jnp.cos(ang)
    sin = jnp.sin(ang)
    return jnp.concatenate([cos, cos], axis=1), jnp.concatenate([-sin, sin], axis=1)


def _overlap_matrix(seq):
    n_win = seq // CMP_STRIDE
    n_cmp = n_win - CMP_BLOCK // CMP_STRIDE + 1
    sj = np.arange(seq // SLC_BLOCK)[:, None]
    ci = np.arange(n_win)[None, :]
    ov = ((ci * CMP_STRIDE <= (sj + 1) * SLC_BLOCK - 1)
          & (ci * CMP_STRIDE + CMP_BLOCK - 1 >= sj * SLC_BLOCK) & (ci < n_cmp))
    return jnp.asarray(ov, BF16)


def _nsa_mixer(u, h, w_in, gate_b, cmp_pe, cmp_w1, cmp_b1, cmp_w2, cmp_b2, w_out, ga, gb, batch, seq):
    assert seq // CMP_STRIDE == ATT_TILE and seq % PROJ_ROW_TILE == 0
    q0 = NSA_Q_DIM
    kvd = NSA_KV_DIM

    def part(k):
        return w_in[:, q0 + k * kvd:q0 + (k + 1) * kvd]

    w_qk = jnp.concatenate([w_in[:, :q0], part(2), part(4)], axis=1).astype(BF16)
    w_v = jnp.concatenate([part(3), part(5)], axis=1).astype(BF16)
    per_group = GQA_REP * N_GATES
    w_g = w_in[:, q0 + 6 * kvd:].reshape(D_MODEL, N_KV_HEADS, per_group)
    w_g = jnp.pad(w_g, ((0, 0), (0, 0), (0, LANES - per_group))).reshape(D_MODEL, N_KV_HEADS * LANES)
    b_g = jnp.pad(gate_b.reshape(N_KV_HEADS, per_group), ((0, 0), (0, LANES - per_group)))
    w_c = jnp.concatenate([part(0), part(1), w_g], axis=1).astype(BF16)
    bias_c = jnp.concatenate([jnp.zeros((2 * kvd,), F32), b_g.reshape(-1)])[None, :]
    cos, sin = _rope_tables(seq)

    tn = kvd
    q_scale = HEAD_DIM ** -0.5 * math.log2(math.e)
    oscale_qk = jnp.concatenate([jnp.full((q0,), q_scale, F32), jnp.ones((2 * kvd,), F32)])[None, :]
    qk = _nsa_proj(u, w_qk, jnp.zeros((1, w_qk.shape[1]), F32), oscale_qk, cos, sin, tn=tn, out_dtype=BF16,
                   rope_chunks=tuple(range(w_qk.shape[1] // tn)), gate_chunks=(), seq=seq)
    vt = _nsa_vt(u, w_v)
    cvg = _nsa_proj(u, w_c, bias_c, jnp.ones_like(bias_c), cos, sin, tn=tn, out_dtype=F32,
                    rope_chunks=(0,), gate_chunks=(2,), seq=seq)
    cmp = _compress(cvg, cmp_pe, cmp_w1, cmp_b1[:, None, :], cmp_w2, cmp_b2[:, None, :], batch, seq)
    o = _nsa_attention(qk, vt, cmp, cvg, _overlap_matrix(seq), batch, seq)
    return _out_proj(o, w_out.astype(BF16), h, ga, gb)


def kernel(x, norm_g, ffn_w_up, ffn_conv_w, ffn_conv_b, ffn_w_down, pool_w_in, pool_w_grp, pool_scale,
           pool_w_out, sgu_w_in, sgu_ln_g, sgu_ln_b, sgu_w_s, sgu_b_s, sgu_w_out, nsa_w_in, nsa_gate_b,
           nsa_cmp_pe, nsa_cmp_w1, nsa_cmp_b1, nsa_cmp_w2, nsa_cmp_b2, nsa_w_out):
    batch, seq, d = x.shape
    h = x.reshape(batch * seq, d)

    def gain(i, k):
        return norm_g[i, k][None, :]

    ffn_up, ffn_down = ffn_w_up, ffn_w_down
    pool_in, pool_grp, pool_out = pool_w_in.astype(BF16), pool_w_grp.astype(BF16), pool_w_out.astype(BF16)
    gains = norm_g[:, :, None, :]
    conv_b = ffn_conv_b[:, None, :]

    u = gain(0, 0)
    for i in range(DEPTH):
        kind, j = i % 3, i // 3
        ga, gb = gain(i, 1), gain(i, 2)
        if kind == 0:
            h, u = _pool_mixer(u, h, pool_in, pool_grp, pool_scale[j][None, :], pool_out, ga, gb, j, seq)
        elif kind == 1:
            h, u = _sgu_mixer(u, h, sgu_w_in[j].astype(BF16), sgu_ln_g[j][None, :], sgu_ln_b[j][None, :],
                              sgu_w_s[j], sgu_b_s[j].T, sgu_w_out[j].astype(BF16), ga, gb)
        else:
            h, u = _nsa_mixer(u, h, nsa_w_in[j], nsa_gate_b[j], nsa_cmp_pe[j], nsa_cmp_w1[j],
                              nsa_cmp_b1[j], nsa_cmp_w2[j], nsa_cmp_b2[j], nsa_w_out[j], ga, gb,
                              batch, seq)
        next_gain = (i + 1, 0) if i + 1 < DEPTH else (i, 3)
        h, u = _conv_ffn(u, h, ffn_up, ffn_conv_w, conv_b, ffn_down, gains, i, next_gain, seq)
    return h.reshape(batch, seq, d)
```

```python
import functools
import math

import jax
import jax.numpy as jnp
import numpy as np
from jax import lax
from jax.experimental import pallas as pl
from jax.experimental.pallas import tpu as pltpu

F32 = jnp.float32
BF16 = jnp.bfloat16

D_MODEL = 2048
DEPTH = 4
RMS_EPS = 1e-6
LN_EPS = 1e-5
NEG = -1e30
BIG = 1e30

FFN_DIM = 5632
CONV_WIDTH = 3
POOL_WINDOWS = (2, 4, 8, 16)
POOL_GROUP_DIM = D_MODEL // len(POOL_WINDOWS)
SGU_CHUNK = 128
SGU_GROUPS = 16
SGU_GROUP_DIM = D_MODEL // SGU_GROUPS

HEAD_DIM = 128
N_HEADS = 16
N_KV_HEADS = 4
GQA_REP = N_HEADS // N_KV_HEADS
ROPE_THETA = 10000.0
CMP_BLOCK = 32
CMP_STRIDE = 16
CMP_HIDDEN = 2 * HEAD_DIM
SLC_BLOCK = 64
SLC_TOPK = 16
WIN = 512
NSA_Q_DIM = N_HEADS * HEAD_DIM
NSA_KV_DIM = N_KV_HEADS * HEAD_DIM
N_GATES = 3

LANES = 128
SUBLANES = 8
BF16_SUBLANES = 16
VMEM_LIMIT = 56 * 1024 * 1024
FFN_VMEM_LIMIT = 60 * 1024 * 1024

ROW_TILE = 512
PROJ_ROW_TILE = 1024
MIX_ROW_TILE = 256
FFN_ROW_TILE = 1024
FFN_COL_TILE = 512
ATT_TILE = 128
ATT_K_STEP = 256
ATT_Q_TILE = 512


def _cparams(*sem):
    return pltpu.CompilerParams(dimension_semantics=sem, vmem_limit_bytes=VMEM_LIMIT)


def _const_spec(shape):
    nd = len(shape)
    return pl.BlockSpec(shape, lambda *_: (0,) * nd, pipeline_mode=pl.Buffered(1))


def _layer_spec(stacked_shape, layer):
    nd = len(stacked_shape) - 1
    return pl.BlockSpec((None,) + tuple(stacked_shape[1:]), lambda *_: (layer,) + (0,) * nd,
                        pipeline_mode=pl.Buffered(1))


def _rms(x):
    return x * lax.rsqrt(jnp.mean(x * x, axis=-1, keepdims=True) + RMS_EPS)


def _residual_update(m, h_ref, ga_ref, gb_ref, ho_ref, uo_ref):
    hn = h_ref[...] + _rms(m) * ga_ref[...]
    ho_ref[...] = hn
    uo_ref[...] = (_rms(hn) * gb_ref[...]).astype(BF16)


def _dot(a, b):
    return jnp.dot(a, b, preferred_element_type=F32)


def _out_proj_kernel(x_ref, w_ref, h_ref, ga_ref, gb_ref, ho_ref, uo_ref):
    _residual_update(_dot(x_ref[...], w_ref[...]), h_ref, ga_ref, gb_ref, ho_ref, uo_ref)


def _out_proj(xin, w, h, ga, gb):
    n, k = xin.shape
    tm = ROW_TILE
    row = lambda i: (i, 0)
    return pl.pallas_call(
        _out_proj_kernel,
        grid=(n // tm,),
        in_specs=[pl.BlockSpec((tm, k), row), _const_spec((k, D_MODEL)),
                  pl.BlockSpec((tm, D_MODEL), row),
                  pl.BlockSpec((1, D_MODEL), lambda i: (0, 0)),
                  pl.BlockSpec((1, D_MODEL), lambda i: (0, 0))],
        out_specs=[pl.BlockSpec((tm, D_MODEL), row), pl.BlockSpec((tm, D_MODEL), row)],
        out_shape=[jax.ShapeDtypeStruct((n, D_MODEL), F32),
                   jax.ShapeDtypeStruct((n, D_MODEL), BF16)],
        compiler_params=_cparams("parallel"),
        name="out_proj",
    )(xin, w, h, ga, gb)


def _ffn_kernel(u_ref, wa_ref, wb_ref, cw_ref, cb_ref, wd_ref, h_hbm, ga_ref, gb_ref,
                ho_hbm, uo_ref, tail_ref, acc_ref, hbuf_ref, sem_ref, *, seq_tiles):
    i = pl.program_id(0)
    c = pl.program_id(1)
    n_i = pl.num_programs(0)
    n_c = pl.num_programs(1)
    tm = u_ref.shape[0]
    keep = tail_ref.shape[1]

    def rows(tile):
        return pl.ds(pl.multiple_of(tile * tm, tm), tm)

    def h_load():
        return pltpu.make_async_copy(h_hbm.at[rows(i)], hbuf_ref, sem_ref.at[0])

    def h_store(tile):
        return pltpu.make_async_copy(hbuf_ref, ho_hbm.at[rows(tile)], sem_ref.at[1])

    @pl.when(c == 1)
    def _():
        @pl.when(i > 0)
        def _():
            h_store(i - 1).wait()
        h_load().start()

    @pl.when(i % seq_tiles == 0)
    def _():
        tail_ref[c] = jnp.zeros(tail_ref.shape[1:], F32)

    @pl.when(c == 0)
    def _():
        acc_ref[...] = jnp.zeros_like(acc_ref)

    u = u_ref[...]
    a = _dot(u, wa_ref[...].astype(BF16))
    b = _dot(u, wb_ref[...].astype(BF16))
    prev = tail_ref[c]
    tail_ref[c] = a[tm - keep:tm]
    row = lax.broadcasted_iota(jnp.int32, a.shape, 0)
    last1 = prev[keep - 1:keep]
    last2 = prev[keep - 2:keep - 1]
    p1 = jnp.where(row == 0, last1, pltpu.roll(a, 1, 0))
    p2 = jnp.where(row == 0, last2, jnp.where(row == 1, last1, pltpu.roll(a, 2, 0)))
    cw = cw_ref[...]
    y = cw[0:1] * p2 + cw[1:2] * p1 + cw[2:3] * a + cb_ref[...]
    gated = y / (1.0 + jnp.exp(-y)) * b
    acc_ref[...] += _dot(gated.astype(BF16), wd_ref[...].astype(BF16))

    @pl.when(c == n_c - 1)
    def _():
        h_load().wait()
        _residual_update(acc_ref[...], hbuf_ref, ga_ref, gb_ref, hbuf_ref, uo_ref)
        h_store(i).start()

        @pl.when(i == n_i - 1)
        def _():
            h_store(i).wait()


def _conv_ffn(u, h, w_up, conv_w, conv_b, w_down, norm_g, layer, next_gain, seq):
    n = u.shape[0]
    tm, tf = FFN_ROW_TILE, FFN_COL_TILE
    n_chunks = FFN_DIM // tf
    row = lambda i, c: (i, 0)

    def row_spec(buffers=1):
        return pl.BlockSpec((tm, D_MODEL), row, pipeline_mode=pl.Buffered(buffers))

    def gain_spec(lyr, slot):
        return pl.BlockSpec((None, None, 1, D_MODEL), lambda i, c: (lyr, slot, 0, 0))

    return pl.pallas_call(
        functools.partial(_ffn_kernel, seq_tiles=seq // tm),
        grid=(n // tm, n_chunks),
        in_specs=[row_spec(),
                  pl.BlockSpec((None, D_MODEL, tf), lambda i, c: (layer, 0, c)),
                  pl.BlockSpec((None, D_MODEL, tf), lambda i, c: (layer, 0, c + n_chunks)),
                  pl.BlockSpec((None, CONV_WIDTH, tf), lambda i, c: (layer, 0, c)),
                  pl.BlockSpec((None, 1, tf), lambda i, c: (layer, 0, c)),
                  pl.BlockSpec((None, tf, D_MODEL), lambda i, c: (layer, c, 0)),
                  pl.BlockSpec(memory_space=pl.ANY),
                  gain_spec(layer, 3),
                  gain_spec(*next_gain)],
        out_specs=[pl.BlockSpec(memory_space=pl.ANY), row_spec(2)],
        out_shape=[jax.ShapeDtypeStruct((n, D_MODEL), F32),
                   jax.ShapeDtypeStruct((n, D_MODEL), BF16)],
        scratch_shapes=[pltpu.VMEM((n_chunks, SUBLANES, tf), F32),
                        pltpu.VMEM((tm, D_MODEL), F32),
                        pltpu.VMEM((tm, D_MODEL), F32),
                        pltpu.SemaphoreType.DMA((2,))],
        compiler_params=pltpu.CompilerParams(dimension_semantics=("arbitrary", "arbitrary"),
                                             vmem_limit_bytes=FFN_VMEM_LIMIT),
        name="conv_ffn",
    )(u, w_up, w_up, conv_w, conv_b, w_down, h, norm_g, norm_g)


def _pool_kernel(u_ref, win_ref, wgrp_ref, scale_ref, wout_ref, h_ref, ga_ref, gb_ref,
                 ho_ref, uo_ref, tail_ref, m_ref, *, seq_tiles, norm_input):
    i = pl.program_id(0)
    tm = h_ref.shape[0]
    halo = tail_ref.shape[0]
    assert halo >= max(POOL_WINDOWS) - 1

    @pl.when(i % seq_tiles == 0)
    def _():
        tail_ref[...] = jnp.zeros_like(tail_ref)

    u = (_rms(h_ref[...]) * u_ref[...]).astype(BF16) if norm_input else u_ref[...]
    tpos = (i % seq_tiles) * tm + lax.broadcasted_iota(jnp.int32, (tm, 1), 0)
    for g, w in enumerate(POOL_WINDOWS):
        cols = slice(g * POOL_GROUP_DIM, (g + 1) * POOL_GROUP_DIM)
        z = _dot(u, win_ref[:, cols])
        x = jnp.concatenate([tail_ref[:, cols], z], axis=0)
        tail_ref[:, cols] = z[tm - halo:]
        s = x
        k = 1
        while k < w:
            s = s + pltpu.roll(s, k, 0)
            k *= 2
        cnt = jnp.minimum(tpos + 1, w).astype(F32)
        p = s[halo:] / cnt - z
        mg = _dot(p.astype(BF16), wgrp_ref[g]) * scale_ref[:, cols]
        m_ref[:, cols] = mg.astype(BF16)
    _residual_update(_dot(m_ref[...], wout_ref[...]), h_ref, ga_ref, gb_ref, ho_ref, uo_ref)


def _pool_mixer(u, h, w_in, w_grp, scale, w_out, ga, gb, layer, seq):
    n = h.shape[0]
    tm = ROW_TILE
    row = lambda i: (i, 0)
    norm_input = u.shape[0] == 1
    u_spec = _const_spec((1, D_MODEL)) if norm_input else pl.BlockSpec((tm, D_MODEL), row)
    return pl.pallas_call(
        functools.partial(_pool_kernel, seq_tiles=seq // tm, norm_input=norm_input),
        grid=(n // tm,),
        in_specs=[u_spec,
                  _layer_spec(w_in.shape, layer),
                  _layer_spec(w_grp.shape, layer),
                  _const_spec((1, D_MODEL)),
                  _layer_spec(w_out.shape, layer),
                  pl.BlockSpec((tm, D_MODEL), row),
                  _const_spec((1, D_MODEL)),
                  _const_spec((1, D_MODEL))],
        out_specs=[pl.BlockSpec((tm, D_MODEL), row), pl.BlockSpec((tm, D_MODEL), row)],
        out_shape=[jax.ShapeDtypeStruct((n, D_MODEL), F32),
                   jax.ShapeDtypeStruct((n, D_MODEL), BF16)],
        scratch_shapes=[pltpu.VMEM((BF16_SUBLANES, D_MODEL), F32), pltpu.VMEM((tm, D_MODEL), BF16)],
        compiler_params=_cparams("arbitrary"),
        name="pool_mixer",
    )(u, w_in, w_grp, scale, w_out, h, ga, gb)


def _gelu_tanh(x):
    c = math.sqrt(2.0 / math.pi)
    return 0.5 * x * (1.0 + jnp.tanh(c * (x + 0.044715 * (x * x * x))))


def _sgu_kernel(u_ref, win_ref, lng_ref, lnb_ref, ws_ref, bst_ref, wout_ref, h_ref, ga_ref, gb_ref,
                ho_ref, uo_ref, gated_ref):
    tm = u_ref.shape[0]
    y = _gelu_tanh(_dot(u_ref[...], win_ref[...]))
    uu = y[:, :D_MODEL]
    v = y[:, D_MODEL:]
    mu = jnp.mean(v, axis=-1, keepdims=True)
    vc = v - mu
    var = jnp.mean(vc * vc, axis=-1, keepdims=True)
    vn = (vc * lax.rsqrt(var + LN_EPS) * lng_ref[...] + lnb_ref[...]).astype(BF16)
    t_idx = lax.broadcasted_iota(jnp.int32, (SGU_CHUNK, SGU_CHUNK), 0)
    s_idx = lax.broadcasted_iota(jnp.int32, (SGU_CHUNK, SGU_CHUNK), 1)
    causal = s_idx <= t_idx
    bst = bst_ref[...]
    for g in range(SGU_GROUPS):
        cols = slice(g * SGU_GROUP_DIM, (g + 1) * SGU_GROUP_DIM)
        ws = jnp.where(causal, ws_ref[g], 0.0).astype(BF16)
        bias = bst[:, g:g + 1]
        for ci in range(tm // SGU_CHUNK):
            rows = slice(ci * SGU_CHUNK, (ci + 1) * SGU_CHUNK)
            mixed = _dot(ws, vn[rows, cols]) + bias
            gated_ref[rows, cols] = (uu[rows, cols] * mixed).astype(BF16)
    _residual_update(_dot(gated_ref[...], wout_ref[...]), h_ref, ga_ref, gb_ref, ho_ref, uo_ref)


def _sgu_mixer(u, h, w_in, ln_g, ln_b, w_s, b_s_t, w_out, ga, gb):
    n = u.shape[0]
    tm = MIX_ROW_TILE
    row = lambda i: (i, 0)
    return pl.pallas_call(
        _sgu_kernel,
        grid=(n // tm,),
        in_specs=[pl.BlockSpec((tm, D_MODEL), row),
                  _const_spec((D_MODEL, 2 * D_MODEL)),
                  _const_spec((1, D_MODEL)),
                  _const_spec((1, D_MODEL)),
                  _const_spec(w_s.shape),
                  _const_spec(b_s_t.shape),
                  _const_spec((D_MODEL, D_MODEL)),
                  pl.BlockSpec((tm, D_MODEL), row),
                  _const_spec((1, D_MODEL)),
                  _const_spec((1, D_MODEL))],
        out_specs=[pl.BlockSpec((tm, D_MODEL), row), pl.BlockSpec((tm, D_MODEL), row)],
        out_shape=[jax.ShapeDtypeStruct((n, D_MODEL), F32),
                   jax.ShapeDtypeStruct((n, D_MODEL), BF16)],
        scratch_shapes=[pltpu.VMEM((tm, D_MODEL), BF16)],
        compiler_params=_cparams("parallel"),
        name="sgu_mixer",
    )(u, w_in, ln_g, ln_b, w_s, b_s_t, w_out, h, ga, gb)


def _nsa_proj_kernel(u_ref, w_ref, wg_ref, bias_ref, oscale_ref, cos_ref, sin_ref, o_ref, *,
                     rope_chunks, gate_chunk):
    j = pl.program_id(1)
    tn = o_ref.shape[1]

    def is_in(chunks):
        hit = j == chunks[0]
        for c in chunks[1:]:
            hit = hit | (j == c)
        return hit

    def project():
        return _dot(u_ref[...], w_ref[...].astype(BF16))

    rope = is_in(rope_chunks)
    plain = jnp.logical_not(rope)
    if gate_chunk is not None:
        plain = plain & (j != gate_chunk)

        @pl.when(j == gate_chunk)
        def _():
            acc = _dot(u_ref[...], wg_ref[...])
            o_ref[...] = (1.0 / (1.0 + jnp.exp(-(acc + bias_ref[...])))).astype(o_ref.dtype)

    @pl.when(rope)
    def _():
        acc = project()
        cos = cos_ref[...]
        sin = sin_ref[...]
        for hh in range(tn // HEAD_DIM):
            cols = slice(hh * HEAD_DIM, (hh + 1) * HEAD_DIM)
            x = acc[:, cols]
            roped = x * cos + pltpu.roll(x, HEAD_DIM // 2, 1) * sin
            o_ref[:, cols] = (roped * oscale_ref[:, cols]).astype(o_ref.dtype)

    @pl.when(plain)
    def _():
        o_ref[...] = project().astype(o_ref.dtype)


def _nsa_proj(u, w_in, w_gate, bias, oscale, cos, sin, *, tn, w_chunk, out_dtype, rope_chunks,
              gate_chunk, seq):
    n = u.shape[0]
    ncols = bias.shape[1]
    tm = PROJ_ROW_TILE
    seq_tiles = seq // tm
    return pl.pallas_call(
        functools.partial(_nsa_proj_kernel, rope_chunks=rope_chunks, gate_chunk=gate_chunk),
        grid=(n // tm, ncols // tn),
        in_specs=[pl.BlockSpec((tm, D_MODEL), lambda i, j: (i, 0)),
                  pl.BlockSpec((D_MODEL, tn), lambda i, j: (0, w_chunk(j))),
                  pl.BlockSpec(w_gate.shape, lambda i, j: (0, 0)),
                  pl.BlockSpec((1, tn), lambda i, j: (0, j)),
                  pl.BlockSpec((1, tn), lambda i, j: (0, j)),
                  pl.BlockSpec((tm, HEAD_DIM), lambda i, j: (i % seq_tiles, 0)),
                  pl.BlockSpec((tm, HEAD_DIM), lambda i, j: (i % seq_tiles, 0))],
        out_specs=pl.BlockSpec((tm, tn), lambda i, j: (i, j)),
        out_shape=jax.ShapeDtypeStruct((n, ncols), out_dtype),
        compiler_params=_cparams("parallel", "arbitrary"),
        name="nsa_proj",
    )(u, w_in, w_gate, bias, oscale, cos, sin)


def _nsa_vt_kernel(u_ref, w_ref, o_ref):
    acc = _dot(u_ref[...], w_ref[...].astype(BF16))
    for kt in range(acc.shape[0] // ATT_TILE):
        for g in range(N_KV_HEADS):
            tile = acc[kt * ATT_TILE:(kt + 1) * ATT_TILE, g * HEAD_DIM:(g + 1) * HEAD_DIM]
            o_ref[kt, g] = tile.T.astype(o_ref.dtype)


def _nsa_vt(u, w_in, w_chunk):
    n = u.shape[0]
    tm = PROJ_ROW_TILE
    kt = tm // ATT_TILE
    return pl.pallas_call(
        _nsa_vt_kernel,
        grid=(n // tm, 2),
        in_specs=[pl.BlockSpec((tm, D_MODEL), lambda i, j: (i, 0)),
                  pl.BlockSpec((D_MODEL, NSA_KV_DIM), lambda i, j: (0, w_chunk(j)))],
        out_specs=pl.BlockSpec((None, kt, N_KV_HEADS, HEAD_DIM, ATT_TILE), lambda i, j: (j, i, 0, 0, 0)),
        out_shape=jax.ShapeDtypeStruct((2, n // ATT_TILE, N_KV_HEADS, HEAD_DIM, ATT_TILE), BF16),
        compiler_params=_cparams("parallel", "arbitrary"),
        name="nsa_vt",
    )(u, w_in)


def _compress_kernel(a_ref, pe_ref, w1_ref, b1_ref, w2_ref, b2_ref, o_ref):
    nb = a_ref.shape[0] // CMP_STRIDE
    top = jnp.zeros((nb, CMP_HIDDEN), F32)
    bot = jnp.zeros((nb, CMP_HIDDEN), F32)
    for p in range(CMP_STRIDE):
        xp = a_ref[pl.ds(p, nb, stride=CMP_STRIDE), :]
        lo = slice(p * HEAD_DIM, (p + 1) * HEAD_DIM)
        hi = slice((CMP_STRIDE + p) * HEAD_DIM, (CMP_STRIDE + p + 1) * HEAD_DIM)
        top += _dot((xp + pe_ref[p:p + 1, :]).astype(BF16), w1_ref[lo, :].astype(BF16))
        bot += _dot((xp + pe_ref[CMP_STRIDE + p:CMP_STRIDE + p + 1, :]).astype(BF16),
                    w1_ref[hi, :].astype(BF16))
    hid = _gelu_tanh(top + pltpu.roll(bot, nb - 1, 0) + b1_ref[...])
    res = _dot(hid.astype(BF16), w2_ref[...].astype(BF16)) + b2_ref[...]

    @pl.when(pl.program_id(0) == 0)
    def _():
        o_ref[...] = res.astype(o_ref.dtype)

    @pl.when(pl.program_id(0) == 1)
    def _():
        o_ref[...] = res.T.astype(o_ref.dtype)


def _compress(cv, pe, w1, b1, w2, b2, batch, seq):
    nb = seq // CMP_STRIDE
    return pl.pallas_call(
        _compress_kernel,
        grid=(2, batch, N_KV_HEADS),
        in_specs=[pl.BlockSpec((seq, HEAD_DIM), lambda s, b, g: (b, s * N_KV_HEADS + g)),
                  pl.BlockSpec((None, CMP_BLOCK, HEAD_DIM), lambda s, b, g: (s, 0, 0)),
                  pl.BlockSpec((None, CMP_BLOCK * HEAD_DIM, CMP_HIDDEN), lambda s, b, g: (s, 0, 0)),
                  pl.BlockSpec((None, 1, CMP_HIDDEN), lambda s, b, g: (s, 0, 0)),
                  pl.BlockSpec((None, CMP_HIDDEN, HEAD_DIM), lambda s, b, g: (s, 0, 0)),
                  pl.BlockSpec((None, 1, HEAD_DIM), lambda s, b, g: (s, 0, 0))],
        out_specs=pl.BlockSpec((None, None, None, nb, HEAD_DIM), lambda s, b, g: (b, s, g, 0, 0)),
        out_shape=jax.ShapeDtypeStruct((batch, 2, N_KV_HEADS, nb, HEAD_DIM), BF16),
        compiler_params=_cparams("parallel", "parallel", "parallel"),
        name="nsa_compress",
    )(cv, pe, w1, b1, w2, b2)


def _attn_update(carry, qt, branches, tq, slab, first_query=0):
    m, l, acc = carry
    ms, ls, accs = [], [], []
    for b, (k, vt, fix) in enumerate(branches):
        for r in range(GQA_REP):
            for lo in range(0, tq, slab):
                q0 = r * tq + lo
                c0 = b * GQA_REP * tq + q0
                if lo + slab <= first_query:
                    ms.append(m[:, c0:c0 + slab])
                    ls.append(l[:, c0:c0 + slab])
                    accs.append(acc[:, c0:c0 + slab])
                    continue
                z = fix(_dot(k, qt[:, q0:q0 + slab]), lo)
                m_old = m[:, c0:c0 + slab]
                m_new = jnp.maximum(m_old, jnp.max(z, axis=0, keepdims=True))
                alpha = jnp.exp2(m_old - m_new)
                p = jnp.exp2(z - m_new)
                ms.append(m_new)
                ls.append(alpha * l[:, c0:c0 + slab] + jnp.sum(p, axis=0, keepdims=True))
                accs.append(alpha * acc[:, c0:c0 + slab] + _dot(vt, p.astype(BF16)))
    return tuple(jnp.concatenate(x, axis=1) for x in (ms, ls, accs))


def _nsa_attn_kernel(q_ref, ks_ref, kw_ref, vst_ref, vwt_ref, kc_ref, vct_ref, gate_ref, ovt_ref,
                     o_ref, bias_ref, *, n_cmp):
    i = pl.program_id(2)
    tq = q_ref.shape[0]
    tk = ATT_K_STEP
    nq = GQA_REP * tq
    n_slc, n_win = ovt_ref.shape

    q = q_ref[...].astype(F32)
    qt = jnp.concatenate([q[:, r * HEAD_DIM:(r + 1) * HEAD_DIM].T for r in range(GQA_REP)],
                         axis=1).astype(BF16)
    rel = lax.broadcasted_iota(jnp.int32, (tk, tq), 1) - lax.broadcasted_iota(jnp.int32, (tk, tq), 0)

    win_idx = lax.broadcasted_iota(jnp.int32, (n_win, nq), 0)
    t_abs = i * tq + (lax.broadcasted_iota(jnp.int32, (n_win, nq), 1) & (tq - 1))
    sc = _dot(kc_ref[...], qt)
    ok_c = (win_idx * CMP_STRIDE + (CMP_BLOCK - 1) <= t_abs) & (win_idx < n_cmp)
    sc = jnp.where(ok_c, sc, NEG)
    e = jnp.exp2(sc - jnp.max(sc, axis=0, keepdims=True))
    p_c = e * (1.0 / jnp.sum(e, axis=0, keepdims=True))
    p_c = jnp.where(t_abs >= CMP_BLOCK - 1, p_c, 0.0)
    o_c = _dot(vct_ref[...], p_c.astype(BF16))

    p_sum = p_c[:, 0:tq]
    for r in range(1, GQA_REP):
        p_sum = p_sum + p_c[:, r * tq:(r + 1) * tq]
    p_hi = p_sum.astype(BF16)
    p_lo = (p_sum - p_hi.astype(F32)).astype(BF16)
    ovt = ovt_ref[...]
    imp = _dot(ovt, p_hi) + _dot(ovt, p_lo)
    blk = lax.broadcasted_iota(jnp.int32, (n_slc, tq), 0)
    cur = (i * tq + lax.broadcasted_iota(jnp.int32, (n_slc, tq), 1)) >> int(math.log2(SLC_BLOCK))
    forced = (blk == 0) | (blk == cur) | (blk == cur - 1)
    imp = jnp.where(forced, BIG, imp)
    imp = jnp.where(blk <= cur, imp, NEG)
    rank = jnp.zeros((n_slc, tq), jnp.int32)
    for k in range(n_slc):
        other = imp[k:k + 1, :]
        ahead = (other > imp) | ((other == imp) & (blk > k))
        rank = rank + ahead.astype(jnp.int32)
    bias_ref[...] = jnp.where((rank < SLC_TOPK) & (blk <= cur), 0.0, NEG)

    init = (jnp.full((1, nq), NEG, F32), jnp.zeros((1, nq), F32), jnp.zeros((HEAD_DIM, nq), F32))
    blocks_per_tile = tk // SLC_BLOCK
    tiles_per_step = tk // ATT_TILE
    diag0 = i * (tq // tk)
    band0 = jnp.maximum(diag0 - WIN // tk, 0)
    slab = LANES

    def keys(ref, kt):
        return ref[pl.ds(pl.multiple_of(kt * tk, tk), tk), :]

    def values_t(ref, kt):
        return jnp.concatenate([ref[tiles_per_step * kt + j] for j in range(tiles_per_step)], axis=1)

    def block_bias(kt):
        rows = [jnp.broadcast_to(bias_ref[pl.ds(blocks_per_tile * kt + j, 1), :], (SLC_BLOCK, tq))
                for j in range(blocks_per_tile)]
        return jnp.concatenate(rows, axis=0)

    def far_body(kt, c):
        bias = block_bias(kt)
        slc = (keys(ks_ref, kt), values_t(vst_ref, kt), lambda z, lo: z + bias[:, lo:lo + slab])
        return _attn_update(c, qt, [slc], tq, slab)

    def near_body(kt, c):
        bias = block_bias(kt)
        in_band = rel < WIN - (i * tq - kt * tk)
        slc = (keys(ks_ref, kt), values_t(vst_ref, kt), lambda z, lo: z + bias[:, lo:lo + slab])
        win = (keys(kw_ref, kt), values_t(vwt_ref, kt),
               lambda z, lo: jnp.where(in_band[:, lo:lo + slab], z, NEG))
        return _attn_update(c, qt, [slc, win], tq, slab)

    c_s = lax.fori_loop(0, band0, far_body, init)
    both = tuple(jnp.concatenate([s, w], axis=1) for s, w in zip(c_s, init))
    both = lax.fori_loop(band0, diag0, near_body, both)
    for j in range(tq // tk):
        kt = diag0 + j
        bias = block_bias(kt)
        causal = rel >= j * tk
        slc = (keys(ks_ref, kt), values_t(vst_ref, kt),
               lambda z, lo: jnp.where(causal[:, lo:lo + slab], z + bias[:, lo:lo + slab], NEG))
        win = (keys(kw_ref, kt), values_t(vwt_ref, kt),
               lambda z, lo: jnp.where(causal[:, lo:lo + slab], z, NEG))
        both = _attn_update(both, qt, [slc, win], tq, slab, first_query=j * tk)
    _, l_sw, a_sw = both
    o_sw = a_sw * (1.0 / l_sw)
    o_s = o_sw[:, :nq]
    o_w = o_sw[:, nq:]

    gt = gate_ref[...].T

    def gate(k):
        return jnp.concatenate([gt[r * N_GATES + k:r * N_GATES + k + 1, :] for r in range(GQA_REP)],
                               axis=1)

    o_t = gate(0) * o_c + gate(1) * o_s + gate(2) * o_w
    for r in range(GQA_REP):
        o_ref[:, r * HEAD_DIM:(r + 1) * HEAD_DIM] = o_t[:, r * tq:(r + 1) * tq].T.astype(o_ref.dtype)


def _nsa_attention(qk, vt, cmp, cvg, ovt, batch, seq):
    n = qk.shape[0]
    tq = ATT_Q_TILE
    assert tq % ATT_K_STEP == 0 and tq <= WIN and WIN % ATT_K_STEP == 0
    assert ATT_K_STEP % ATT_TILE == 0 and ATT_TILE % SLC_BLOCK == 0
    qt = seq // tq
    gw = GQA_REP * HEAD_DIM
    k0 = NSA_Q_DIM // HEAD_DIM
    gate0 = 2 * N_KV_HEADS
    n_win = seq // CMP_STRIDE

    def k_spec(which):
        return pl.BlockSpec((seq, HEAD_DIM), lambda b, g, i: (b, k0 + which * N_KV_HEADS + g))

    def vt_spec(which):
        return pl.BlockSpec((None, seq // ATT_TILE, None, HEAD_DIM, ATT_TILE),
                            lambda b, g, i: (which, b, g, 0, 0))

    def cmp_spec(which):
        return pl.BlockSpec((None, None, None, n_win, HEAD_DIM), lambda b, g, i: (b, which, g, 0, 0))

    return pl.pallas_call(
        functools.partial(_nsa_attn_kernel, n_cmp=n_win - CMP_BLOCK // CMP_STRIDE + 1),
        grid=(batch, N_KV_HEADS, qt),
        in_specs=[pl.BlockSpec((tq, gw), lambda b, g, i: (b * qt + i, g)),
                  k_spec(0), k_spec(1), vt_spec(0), vt_spec(1),
                  cmp_spec(0), cmp_spec(1),
                  pl.BlockSpec((tq, LANES), lambda b, g, i: (b * qt + i, gate0 + g)),
                  pl.BlockSpec(ovt.shape, lambda b, g, i: (0, 0))],
        out_specs=pl.BlockSpec((tq, gw), lambda b, g, i: (b * qt + i, g)),
        out_shape=jax.ShapeDtypeStruct((n, NSA_Q_DIM), BF16),
        scratch_shapes=[pltpu.VMEM(ovt.shape[:1] + (tq,), F32)],
        compiler_params=_cparams("parallel", "parallel", "arbitrary"),
        name="nsa_attention",
    )(qk, qk, qk, vt, vt, cmp, cmp, cvg, ovt)


def _rope_tables(seq):
    half = HEAD_DIM // 2
    inv = 1.0 / (ROPE_THETA ** (jnp.arange(half, dtype=F32) / half))
    ang = jnp.arange(seq, dtype=F32)[:, None] * inv[None, :]
    cos = jnp.cos(ang)
    sin = jnp.sin(ang)
    return jnp.concatenate([cos, cos], axis=1), jnp.concatenate([-sin, sin], axis=1)


def _overlap_matrix(seq):
    n_win = seq // CMP_STRIDE
    n_cmp = n_win - CMP_BLOCK // CMP_STRIDE + 1
    sj = np.arange(seq // SLC_BLOCK)[:, None]
    ci = np.arange(n_win)[None, :]
    ov = ((ci * CMP_STRIDE <= (sj + 1) * SLC_BLOCK - 1)
          & (ci * CMP_STRIDE + CMP_BLOCK - 1 >= sj * SLC_BLOCK) & (ci < n_cmp))
    return jnp.asarray(ov, BF16)


def _nsa_mixer(u, h, w_in, gate_b, cmp_pe, cmp_w1, cmp_b1, cmp_w2, cmp_b2, w_out, ga, gb, batch, seq):
    assert seq // CMP_STRIDE == ATT_TILE and seq % PROJ_ROW_TILE == 0
    q0 = NSA_Q_DIM
    kvd = NSA_KV_DIM
    tn = kvd
    n_q = q0 // tn
    kc, vc, ks, vs, kw, vw = range(n_q, n_q + 6)
    per_group = GQA_REP * N_GATES
    w_g = w_in[:, q0 + 6 * kvd:].reshape(D_MODEL, N_KV_HEADS, per_group)
    w_g = jnp.pad(w_g, ((0, 0), (0, 0), (0, LANES - per_group))).reshape(D_MODEL, N_KV_HEADS * LANES)
    w_g = w_g.astype(BF16)
    b_g = jnp.pad(gate_b.reshape(N_KV_HEADS, per_group), ((0, 0), (0, LANES - per_group)))
    bias_c = jnp.concatenate([jnp.zeros((2 * kvd,), F32), b_g.reshape(-1)])[None, :]
    cos, sin = _rope_tables(seq)

    q_scale = HEAD_DIM ** -0.5 * math.log2(math.e)
    oscale_qk = jnp.concatenate([jnp.full((q0,), q_scale, F32), jnp.ones((2 * kvd,), F32)])[None, :]
    qk = _nsa_proj(u, w_in, w_g, jnp.zeros_like(oscale_qk), oscale_qk, cos, sin, tn=tn,
                   w_chunk=lambda j: jnp.where(j < n_q, j, ks + (kw - ks) * (j - n_q)),
                   out_dtype=BF16, rope_chunks=tuple(range(n_q + 2)), gate_chunk=None, seq=seq)
    vt = _nsa_vt(u, w_in, lambda j: vs + (vw - vs) * j)
    cvg = _nsa_proj(u, w_in, w_g, bias_c, jnp.ones_like(bias_c), cos, sin, tn=tn,
                    w_chunk=lambda j: jnp.minimum(kc + j, vc),
                    out_dtype=F32, rope_chunks=(0,), gate_chunk=2, seq=seq)
    cmp = _compress(cvg, cmp_pe, cmp_w1, cmp_b1[:, None, :], cmp_w2, cmp_b2[:, None, :], batch, seq)
    o = _nsa_attention(qk, vt, cmp, cvg, _overlap_matrix(seq), batch, seq)
    return _out_proj(o, w_out.astype(BF16), h, ga, gb)


def kernel(x, norm_g, ffn_w_up, ffn_conv_w, ffn_conv_b, ffn_w_down, pool_w_in, pool_w_grp, pool_scale,
           pool_w_out, sgu_w_in, sgu_ln_g, sgu_ln_b, sgu_w_s, sgu_b_s, sgu_w_out, nsa_w_in, nsa_gate_b,
           nsa_cmp_pe, nsa_cmp_w1, nsa_cmp_b1, nsa_cmp_w2, nsa_cmp_b2, nsa_w_out):
    batch, seq, d = x.shape
    h = x.reshape(batch * seq, d)

    def gain(i, k):
        return norm_g[i, k][None, :]

    ffn_up, ffn_down = ffn_w_up, ffn_w_down
    pool_in, pool_grp, pool_out = pool_w_in.astype(BF16), pool_w_grp.astype(BF16), pool_w_out.astype(BF16)
    gains = norm_g[:, :, None, :]
    conv_b = ffn_conv_b[:, None, :]

    u = gain(0, 0)
    for i in range(DEPTH):
        kind, j = i % 3, i // 3
        ga, gb = gain(i, 1), gain(i, 2)
        if kind == 0:
            h, u = _pool_mixer(u, h, pool_in, pool_grp, pool_scale[j][None, :], pool_out, ga, gb, j, seq)
        elif kind == 1:
            h, u = _sgu_mixer(u, h, sgu_w_in[j].astype(BF16), sgu_ln_g[j][None, :], sgu_ln_b[j][None, :],
                              sgu_w_s[j], sgu_b_s[j].T, sgu_w_out[j].astype(BF16), ga, gb)
        else:
            h, u = _nsa_mixer(u, h, nsa_w_in[j], nsa_gate_b[j], nsa_cmp_pe[j], nsa_cmp_w1[j],
                              nsa_cmp_b1[j], nsa_cmp_w2[j], nsa_cmp_b2[j], nsa_w_out[j], ga, gb,
                              batch, seq)
        next_gain = (i + 1, 0) if i + 1 < DEPTH else (i, 3)
        h, u = _conv_ffn(u, h, ffn_up, ffn_conv_w, conv_b, ffn_down, gains, i, next_gain, seq)
    return h.reshape(batch, seq, d)
```

```python
import functools
import math

import jax
import jax.numpy as jnp
import numpy as np
from jax import lax
from jax.experimental import pallas as pl
from jax.experimental.pallas import tpu as pltpu

F32 = jnp.float32
BF16 = jnp.bfloat16

D_MODEL = 2048
DEPTH = 4
RMS_EPS = 1e-6
LN_EPS = 1e-5
NEG = -1e30
BIG = 1e30

FFN_DIM = 5632
CONV_WIDTH = 3
POOL_WINDOWS = (2, 4, 8, 16)
POOL_GROUP_DIM = D_MODEL // len(POOL_WINDOWS)
SGU_CHUNK = 128
SGU_GROUPS = 16
SGU_GROUP_DIM = D_MODEL // SGU_GROUPS

HEAD_DIM = 128
N_HEADS = 16
N_KV_HEADS = 4
GQA_REP = N_HEADS // N_KV_HEADS
ROPE_THETA = 10000.0
CMP_BLOCK = 32
CMP_STRIDE = 16
CMP_HIDDEN = 2 * HEAD_DIM
SLC_BLOCK = 64
SLC_TOPK = 16
WIN = 512
NSA_Q_DIM = N_HEADS * HEAD_DIM
NSA_KV_DIM = N_KV_HEADS * HEAD_DIM
N_GATES = 3

LANES = 128
SUBLANES = 8
BF16_SUBLANES = 16
VMEM_LIMIT = 56 * 1024 * 1024
FFN_VMEM_LIMIT = 60 * 1024 * 1024

ROW_TILE = 512
PROJ_ROW_TILE = 1024
MIX_ROW_TILE = 256
FFN_ROW_TILE = 1024
FFN_COL_TILE = 512
ATT_TILE = 128
ATT_K_STEP = 256
ATT_Q_TILE = 512


def _cparams(*sem):
    return pltpu.CompilerParams(dimension_semantics=sem, vmem_limit_bytes=VMEM_LIMIT)


def _const_spec(shape):
    nd = len(shape)
    return pl.BlockSpec(shape, lambda *_: (0,) * nd, pipeline_mode=pl.Buffered(1))


def _layer_spec(stacked_shape, layer):
    nd = len(stacked_shape) - 1
    return pl.BlockSpec((None,) + tuple(stacked_shape[1:]), lambda *_: (layer,) + (0,) * nd,
                        pipeline_mode=pl.Buffered(1))


def _rms(x):
    return x * lax.rsqrt(jnp.mean(x * x, axis=-1, keepdims=True) + RMS_EPS)


def _residual_update(m, h_ref, ga_ref, gb_ref, ho_ref, uo_ref):
    hn = h_ref[...] + _rms(m) * ga_ref[...]
    ho_ref[...] = hn
    uo_ref[...] = (_rms(hn) * gb_ref[...]).astype(BF16)


def _dot(a, b):
    return jnp.dot(a, b, preferred_element_type=F32)


def _dot_nt(a, b):
    return lax.dot_general(a, b, (((1,), (1,)), ((), ())), preferred_element_type=F32)


def _out_proj_kernel(x_ref, w_ref, h_ref, ga_ref, gb_ref, ho_ref, uo_ref):
    _residual_update(_dot(x_ref[...], w_ref[...]), h_ref, ga_ref, gb_ref, ho_ref, uo_ref)


def _out_proj(xin, w, h, ga, gb):
    n, k = xin.shape
    tm = ROW_TILE
    row = lambda i: (i, 0)
    return pl.pallas_call(
        _out_proj_kernel,
        grid=(n // tm,),
        in_specs=[pl.BlockSpec((tm, k), row), _const_spec((k, D_MODEL)),
                  pl.BlockSpec((tm, D_MODEL), row),
                  pl.BlockSpec((1, D_MODEL), lambda i: (0, 0)),
                  pl.BlockSpec((1, D_MODEL), lambda i: (0, 0))],
        out_specs=[pl.BlockSpec((tm, D_MODEL), row), pl.BlockSpec((tm, D_MODEL), row)],
        out_shape=[jax.ShapeDtypeStruct((n, D_MODEL), F32),
                   jax.ShapeDtypeStruct((n, D_MODEL), BF16)],
        compiler_params=_cparams("parallel"),
        name="out_proj",
    )(xin, w, h, ga, gb)


def _ffn_kernel(u_ref, wa_ref, wb_ref, cw_ref, cb_ref, wd_ref, h_hbm, ga_ref, gb_ref,
                ho_hbm, uo_hbm, tail_ref, acc_ref, hbuf_ref, ubuf_ref, sem_ref, *, seq_tiles):
    i = pl.program_id(0)
    c = pl.program_id(1)
    n_i = pl.num_programs(0)
    n_c = pl.num_programs(1)
    tm = u_ref.shape[0]
    keep = tail_ref.shape[1]

    def rows(tile):
        return pl.ds(pl.multiple_of(tile * tm, tm), tm)

    def h_load():
        return pltpu.make_async_copy(h_hbm.at[rows(i)], hbuf_ref, sem_ref.at[0])

    def h_store(tile):
        return pltpu.make_async_copy(hbuf_ref, ho_hbm.at[rows(tile)], sem_ref.at[1])

    def u_store(tile):
        return pltpu.make_async_copy(ubuf_ref, uo_hbm.at[rows(tile)], sem_ref.at[2])

    @pl.when(c == 1)
    def _():
        @pl.when(i > 0)
        def _():
            h_store(i - 1).wait()
            u_store(i - 1).wait()
        h_load().start()

    @pl.when(i % seq_tiles == 0)
    def _():
        tail_ref[c] = jnp.zeros(tail_ref.shape[1:], F32)

    @pl.when(c == 0)
    def _():
        acc_ref[...] = jnp.zeros_like(acc_ref)

    u = u_ref[...]
    a = _dot(u, wa_ref[...].astype(BF16))
    b = _dot(u, wb_ref[...].astype(BF16))
    prev = tail_ref[c]
    tail_ref[c] = a[tm - keep:tm]
    row = lax.broadcasted_iota(jnp.int32, a.shape, 0)
    last1 = prev[keep - 1:keep]
    last2 = prev[keep - 2:keep - 1]
    p1 = jnp.where(row == 0, last1, pltpu.roll(a, 1, 0))
    p2 = jnp.where(row == 0, last2, jnp.where(row == 1, last1, pltpu.roll(a, 2, 0)))
    cw = cw_ref[...]
    y = cw[0:1] * p2 + cw[1:2] * p1 + cw[2:3] * a + cb_ref[...]
    gated = y / (1.0 + jnp.exp(-y)) * b
    acc_ref[...] += _dot(gated.astype(BF16), wd_ref[...].astype(BF16))

    @pl.when(c == n_c - 1)
    def _():
        h_load().wait()
        _residual_update(acc_ref[...], hbuf_ref, ga_ref, gb_ref, hbuf_ref, ubuf_ref)
        h_store(i).start()
        u_store(i).start()

        @pl.when(i == n_i - 1)
        def _():
            h_store(i).wait()
            u_store(i).wait()


def _conv_ffn(u, h, w_up, conv_w, conv_b, w_down, norm_g, layer, next_gain, seq):
    n = u.shape[0]
    tm, tf = FFN_ROW_TILE, FFN_COL_TILE
    n_chunks = FFN_DIM // tf
    assert n_chunks >= 2

    def gain_spec(lyr, slot):
        return pl.BlockSpec((None, None, 1, D_MODEL), lambda i, c: (lyr, slot, 0, 0))

    return pl.pallas_call(
        functools.partial(_ffn_kernel, seq_tiles=seq // tm),
        grid=(n // tm, n_chunks),
        in_specs=[pl.BlockSpec((tm, D_MODEL), lambda i, c: (i, 0)),
                  pl.BlockSpec((None, D_MODEL, tf), lambda i, c: (layer, 0, c)),
                  pl.BlockSpec((None, D_MODEL, tf), lambda i, c: (layer, 0, c + n_chunks)),
                  pl.BlockSpec((None, CONV_WIDTH, tf), lambda i, c: (layer, 0, c)),
                  pl.BlockSpec((None, 1, tf), lambda i, c: (layer, 0, c)),
                  pl.BlockSpec((None, tf, D_MODEL), lambda i, c: (layer, c, 0)),
                  pl.BlockSpec(memory_space=pl.ANY),
                  gain_spec(layer, 3),
                  gain_spec(*next_gain)],
        out_specs=[pl.BlockSpec(memory_space=pl.ANY), pl.BlockSpec(memory_space=pl.ANY)],
        out_shape=[jax.ShapeDtypeStruct((n, D_MODEL), F32),
                   jax.ShapeDtypeStruct((n, D_MODEL), BF16)],
        scratch_shapes=[pltpu.VMEM((n_chunks, SUBLANES, tf), F32),
                        pltpu.VMEM((tm, D_MODEL), F32),
                        pltpu.VMEM((tm, D_MODEL), F32),
                        pltpu.VMEM((tm, D_MODEL), BF16),
                        pltpu.SemaphoreType.DMA((3,))],
        compiler_params=pltpu.CompilerParams(dimension_semantics=("arbitrary", "arbitrary"),
                                             vmem_limit_bytes=FFN_VMEM_LIMIT),
        name="conv_ffn",
    )(u, w_up, w_up, conv_w, conv_b, w_down, h, norm_g, norm_g)


def _pool_kernel(u_ref, win_ref, wgrp_ref, scale_ref, wout_ref, h_ref, ga_ref, gb_ref,
                 ho_ref, uo_ref, tail_ref, m_ref, *, seq_tiles, norm_input):
    i = pl.program_id(0)
    tm = h_ref.shape[0]
    halo = tail_ref.shape[0]
    assert halo >= max(POOL_WINDOWS) - 1

    @pl.when(i % seq_tiles == 0)
    def _():
        tail_ref[...] = jnp.zeros_like(tail_ref)

    u = (_rms(h_ref[...]) * u_ref[...]).astype(BF16) if norm_input else u_ref[...]
    tpos = (i % seq_tiles) * tm + lax.broadcasted_iota(jnp.int32, (tm, 1), 0)
    for g, w in enumerate(POOL_WINDOWS):
        cols = slice(g * POOL_GROUP_DIM, (g + 1) * POOL_GROUP_DIM)
        z = _dot(u, win_ref[:, cols])
        x = jnp.concatenate([tail_ref[:, cols], z], axis=0)
        tail_ref[:, cols] = z[tm - halo:]
        s = x
        k = 1
        while k < w:
            s = s + pltpu.roll(s, k, 0)
            k *= 2
        cnt = jnp.minimum(tpos + 1, w).astype(F32)
        p = s[halo:] / cnt - z
        mg = _dot(p.astype(BF16), wgrp_ref[g]) * scale_ref[:, cols]
        m_ref[:, cols] = mg.astype(BF16)
    _residual_update(_dot(m_ref[...], wout_ref[...]), h_ref, ga_ref, gb_ref, ho_ref, uo_ref)


def _pool_mixer(u, h, w_in, w_grp, scale, w_out, ga, gb, layer, seq):
    n = h.shape[0]
    tm = ROW_TILE
    row = lambda i: (i, 0)
    norm_input = u.shape[0] == 1
    u_spec = _const_spec((1, D_MODEL)) if norm_input else pl.BlockSpec((tm, D_MODEL), row)
    return pl.pallas_call(
        functools.partial(_pool_kernel, seq_tiles=seq // tm, norm_input=norm_input),
        grid=(n // tm,),
        in_specs=[u_spec,
                  _layer_spec(w_in.shape, layer),
                  _layer_spec(w_grp.shape, layer),
                  _const_spec((1, D_MODEL)),
                  _layer_spec(w_out.shape, layer),
                  pl.BlockSpec((tm, D_MODEL), row),
                  _const_spec((1, D_MODEL)),
                  _const_spec((1, D_MODEL))],
        out_specs=[pl.BlockSpec((tm, D_MODEL), row), pl.BlockSpec((tm, D_MODEL), row)],
        out_shape=[jax.ShapeDtypeStruct((n, D_MODEL), F32),
                   jax.ShapeDtypeStruct((n, D_MODEL), BF16)],
        scratch_shapes=[pltpu.VMEM((BF16_SUBLANES, D_MODEL), F32), pltpu.VMEM((tm, D_MODEL), BF16)],
        compiler_params=_cparams("arbitrary"),
        name="pool_mixer",
    )(u, w_in, w_grp, scale, w_out, h, ga, gb)


def _gelu_tanh(x):
    c = math.sqrt(2.0 / math.pi)
    return 0.5 * x * (1.0 + jnp.tanh(c * (x + 0.044715 * (x * x * x))))


def _sgu_kernel(u_ref, win_ref, lng_ref, lnb_ref, ws_ref, bst_ref, wout_ref, h_ref, ga_ref, gb_ref,
                ho_ref, uo_ref, gated_ref):
    tm = u_ref.shape[0]
    y = _gelu_tanh(_dot(u_ref[...], win_ref[...]))
    uu = y[:, :D_MODEL]
    v = y[:, D_MODEL:]
    mu = jnp.mean(v, axis=-1, keepdims=True)
    vc = v - mu
    var = jnp.mean(vc * vc, axis=-1, keepdims=True)
    vn = (vc * lax.rsqrt(var + LN_EPS) * lng_ref[...] + lnb_ref[...]).astype(BF16)
    t_idx = lax.broadcasted_iota(jnp.int32, (SGU_CHUNK, SGU_CHUNK), 0)
    s_idx = lax.broadcasted_iota(jnp.int32, (SGU_CHUNK, SGU_CHUNK), 1)
    causal = s_idx <= t_idx
    bst = bst_ref[...]
    for g in range(SGU_GROUPS):
        cols = slice(g * SGU_GROUP_DIM, (g + 1) * SGU_GROUP_DIM)
        ws = jnp.where(causal, ws_ref[g], 0.0).astype(BF16)
        bias = bst[:, g:g + 1]
        for ci in range(tm // SGU_CHUNK):
            rows = slice(ci * SGU_CHUNK, (ci + 1) * SGU_CHUNK)
            mixed = _dot(ws, vn[rows, cols]) + bias
            gated_ref[rows, cols] = (uu[rows, cols] * mixed).astype(BF16)
    _residual_update(_dot(gated_ref[...], wout_ref[...]), h_ref, ga_ref, gb_ref, ho_ref, uo_ref)


def _sgu_mixer(u, h, w_in, ln_g, ln_b, w_s, b_s_t, w_out, ga, gb):
    n = u.shape[0]
    tm = MIX_ROW_TILE
    row = lambda i: (i, 0)
    return pl.pallas_call(
        _sgu_kernel,
        grid=(n // tm,),
        in_specs=[pl.BlockSpec((tm, D_MODEL), row),
                  _const_spec((D_MODEL, 2 * D_MODEL)),
                  _const_spec((1, D_MODEL)),
                  _const_spec((1, D_MODEL)),
                  _const_spec(w_s.shape),
                  _const_spec(b_s_t.shape),
                  _const_spec((D_MODEL, D_MODEL)),
                  pl.BlockSpec((tm, D_MODEL), row),
                  _const_spec((1, D_MODEL)),
                  _const_spec((1, D_MODEL))],
        out_specs=[pl.BlockSpec((tm, D_MODEL), row), pl.BlockSpec((tm, D_MODEL), row)],
        out_shape=[jax.ShapeDtypeStruct((n, D_MODEL), F32),
                   jax.ShapeDtypeStruct((n, D_MODEL), BF16)],
        scratch_shapes=[pltpu.VMEM((tm, D_MODEL), BF16)],
        compiler_params=_cparams("parallel"),
        name="sgu_mixer",
    )(u, w_in, ln_g, ln_b, w_s, b_s_t, w_out, h, ga, gb)


def _nsa_proj_kernel(u_ref, w_ref, wg_ref, bias_ref, oscale_ref, cos_ref, sin_ref, o_ref, *,
                     rope_chunks, gate_chunk):
    j = pl.program_id(1)
    tn = o_ref.shape[1]

    def is_in(chunks):
        hit = j == chunks[0]
        for c in chunks[1:]:
            hit = hit | (j == c)
        return hit

    def project():
        return _dot_nt(u_ref[...], w_ref[...].astype(BF16))

    rope = is_in(rope_chunks)
    plain = jnp.logical_not(rope)
    if gate_chunk is not None:
        plain = plain & (j != gate_chunk)

        @pl.when(j == gate_chunk)
        def _():
            acc = _dot(u_ref[...], wg_ref[...])
            o_ref[...] = (1.0 / (1.0 + jnp.exp(-(acc + bias_ref[...])))).astype(o_ref.dtype)

    @pl.when(rope)
    def _():
        acc = project()
        cos = cos_ref[...]
        sin = sin_ref[...]
        for hh in range(tn // HEAD_DIM):
            cols = slice(hh * HEAD_DIM, (hh + 1) * HEAD_DIM)
            x = acc[:, cols]
            roped = x * cos + pltpu.roll(x, HEAD_DIM // 2, 1) * sin
            o_ref[:, cols] = (roped * oscale_ref[:, cols]).astype(o_ref.dtype)

    @pl.when(plain)
    def _():
        o_ref[...] = project().astype(o_ref.dtype)


def _nsa_proj(u, w_in, w_gate, bias, oscale, cos, sin, *, tn, w_chunk, out_dtype, rope_chunks,
              gate_chunk, seq):
    n = u.shape[0]
    ncols = bias.shape[1]
    tm = PROJ_ROW_TILE
    seq_tiles = seq // tm
    return pl.pallas_call(
        functools.partial(_nsa_proj_kernel, rope_chunks=rope_chunks, gate_chunk=gate_chunk),
        grid=(n // tm, ncols // tn),
        in_specs=[pl.BlockSpec((tm, D_MODEL), lambda i, j: (i, 0)),
                  pl.BlockSpec((tn, D_MODEL), lambda i, j: (w_chunk(j), 0)),
                  pl.BlockSpec(w_gate.shape, lambda i, j: (0, 0)),
                  pl.BlockSpec((1, tn), lambda i, j: (0, j)),
                  pl.BlockSpec((1, tn), lambda i, j: (0, j)),
                  pl.BlockSpec((tm, HEAD_DIM), lambda i, j: (i % seq_tiles, 0)),
                  pl.BlockSpec((tm, HEAD_DIM), lambda i, j: (i % seq_tiles, 0))],
        out_specs=pl.BlockSpec((tm, tn), lambda i, j: (i, j)),
        out_shape=jax.ShapeDtypeStruct((n, ncols), out_dtype),
        compiler_params=_cparams("parallel", "arbitrary"),
        name="nsa_proj",
    )(u, w_in, w_gate, bias, oscale, cos, sin)


def _nsa_vt_kernel(u_ref, w_ref, o_ref):
    acc = _dot_nt(u_ref[...], w_ref[...].astype(BF16))
    for kt in range(acc.shape[0] // ATT_TILE):
        for g in range(N_KV_HEADS):
            tile = acc[kt * ATT_TILE:(kt + 1) * ATT_TILE, g * HEAD_DIM:(g + 1) * HEAD_DIM]
            o_ref[kt, g] = tile.T.astype(o_ref.dtype)


def _nsa_vt(u, w_in, w_chunk):
    n = u.shape[0]
    tm = PROJ_ROW_TILE
    kt = tm // ATT_TILE
    return pl.pallas_call(
        _nsa_vt_kernel,
        grid=(n // tm, 2),
        in_specs=[pl.BlockSpec((tm, D_MODEL), lambda i, j: (i, 0)),
                  pl.BlockSpec((NSA_KV_DIM, D_MODEL), lambda i, j: (w_chunk(j), 0))],
        out_specs=pl.BlockSpec((None, kt, N_KV_HEADS, HEAD_DIM, ATT_TILE), lambda i, j: (j, i, 0, 0, 0)),
        out_shape=jax.ShapeDtypeStruct((2, n // ATT_TILE, N_KV_HEADS, HEAD_DIM, ATT_TILE), BF16),
        compiler_params=_cparams("parallel", "arbitrary"),
        name="nsa_vt",
    )(u, w_in)


def _compress_kernel(a_ref, pe_ref, w1_ref, b1_ref, w2_ref, b2_ref, o_ref):
    nb = a_ref.shape[0] // CMP_STRIDE
    top = jnp.zeros((nb, CMP_HIDDEN), F32)
    bot = jnp.zeros((nb, CMP_HIDDEN), F32)
    for p in range(CMP_STRIDE):
        xp = a_ref[pl.ds(p, nb, stride=CMP_STRIDE), :]
        lo = slice(p * HEAD_DIM, (p + 1) * HEAD_DIM)
        hi = slice((CMP_STRIDE + p) * HEAD_DIM, (CMP_STRIDE + p + 1) * HEAD_DIM)
        top += _dot((xp + pe_ref[p:p + 1, :]).astype(BF16), w1_ref[lo, :].astype(BF16))
        bot += _dot((xp + pe_ref[CMP_STRIDE + p:CMP_STRIDE + p + 1, :]).astype(BF16),
                    w1_ref[hi, :].astype(BF16))
    hid = _gelu_tanh(top + pltpu.roll(bot, nb - 1, 0) + b1_ref[...])
    res = _dot(hid.astype(BF16), w2_ref[...].astype(BF16)) + b2_ref[...]

    @pl.when(pl.program_id(0) == 0)
    def _():
        o_ref[...] = res.astype(o_ref.dtype)

    @pl.when(pl.program_id(0) == 1)
    def _():
        o_ref[...] = res.T.astype(o_ref.dtype)


def _compress(cv, pe, w1, b1, w2, b2, batch, seq):
    nb = seq // CMP_STRIDE
    return pl.pallas_call(
        _compress_kernel,
        grid=(2, batch, N_KV_HEADS),
        in_specs=[pl.BlockSpec((seq, HEAD_DIM), lambda s, b, g: (b, s * N_KV_HEADS + g)),
                  pl.BlockSpec((None, CMP_BLOCK, HEAD_DIM), lambda s, b, g: (s, 0, 0)),
                  pl.BlockSpec((None, CMP_BLOCK * HEAD_DIM, CMP_HIDDEN), lambda s, b, g: (s, 0, 0)),
                  pl.BlockSpec((None, 1, CMP_HIDDEN), lambda s, b, g: (s, 0, 0)),
                  pl.BlockSpec((None, CMP_HIDDEN, HEAD_DIM), lambda s, b, g: (s, 0, 0)),
                  pl.BlockSpec((None, 1, HEAD_DIM), lambda s, b, g: (s, 0, 0))],
        out_specs=pl.BlockSpec((None, None, None, nb, HEAD_DIM), lambda s, b, g: (b, s, g, 0, 0)),
        out_shape=jax.ShapeDtypeStruct((batch, 2, N_KV_HEADS, nb, HEAD_DIM), BF16),
        compiler_params=_cparams("parallel", "parallel", "parallel"),
        name="nsa_compress",
    )(cv, pe, w1, b1, w2, b2)


def _attn_update(carry, qt, branches, tq, slab, first_query=0):
    m, l, acc = carry
    ms, ls, accs = [], [], []
    for b, (k, vt, fix) in enumerate(branches):
        for r in range(GQA_REP):
            for lo in range(0, tq, slab):
                q0 = r * tq + lo
                c0 = b * GQA_REP * tq + q0
                if lo + slab <= first_query:
                    ms.append(m[:, c0:c0 + slab])
                    ls.append(l[:, c0:c0 + slab])
                    accs.append(acc[:, c0:c0 + slab])
                    continue
                z = fix(_dot(k, qt[:, q0:q0 + slab]), lo)
                m_old = m[:, c0:c0 + slab]
                m_new = jnp.maximum(m_old, jnp.max(z, axis=0, keepdims=True))
                alpha = jnp.exp2(m_old - m_new)
                p = jnp.exp2(z - m_new)
                ms.append(m_new)
                ls.append(alpha * l[:, c0:c0 + slab] + jnp.sum(p, axis=0, keepdims=True))
                accs.append(alpha * acc[:, c0:c0 + slab] + _dot(vt, p.astype(BF16)))
    return tuple(jnp.concatenate(x, axis=1) for x in (ms, ls, accs))


def _nsa_attn_kernel(q_ref, ks_ref, kw_ref, vst_ref, vwt_ref, kc_ref, vct_ref, gate_ref, ovt_ref,
                     o_ref, bias_ref, *, n_cmp):
    i = pl.program_id(2)
    tq = q_ref.shape[0]
    tk = ATT_K_STEP
    nq = GQA_REP * tq
    n_slc, n_win = ovt_ref.shape

    q = q_ref[...].astype(F32)
    qt = jnp.concatenate([q[:, r * HEAD_DIM:(r + 1) * HEAD_DIM].T for r in range(GQA_REP)],
                         axis=1).astype(BF16)
    rel = lax.broadcasted_iota(jnp.int32, (tk, tq), 1) - lax.broadcasted_iota(jnp.int32, (tk, tq), 0)

    win_idx = lax.broadcasted_iota(jnp.int32, (n_win, nq), 0)
    t_abs = i * tq + (lax.broadcasted_iota(jnp.int32, (n_win, nq), 1) & (tq - 1))
    sc = _dot(kc_ref[...], qt)
    ok_c = (win_idx * CMP_STRIDE + (CMP_BLOCK - 1) <= t_abs) & (win_idx < n_cmp)
    sc = jnp.where(ok_c, sc, NEG)
    e = jnp.exp2(sc - jnp.max(sc, axis=0, keepdims=True))
    p_c = e * (1.0 / jnp.sum(e, axis=0, keepdims=True))
    p_c = jnp.where(t_abs >= CMP_BLOCK - 1, p_c, 0.0)
    o_c = _dot(vct_ref[...], p_c.astype(BF16))

    p_sum = p_c[:, 0:tq]
    for r in range(1, GQA_REP):
        p_sum = p_sum + p_c[:, r * tq:(r + 1) * tq]
    p_hi = p_sum.astype(BF16)
    p_lo = (p_sum - p_hi.astype(F32)).astype(BF16)
    ovt = ovt_ref[...]
    imp = _dot(ovt, p_hi) + _dot(ovt, p_lo)
    blk = lax.broadcasted_iota(jnp.int32, (n_slc, tq), 0)
    cur = (i * tq + lax.broadcasted_iota(jnp.int32, (n_slc, tq), 1)) >> int(math.log2(SLC_BLOCK))
    forced = (blk == 0) | (blk == cur) | (blk == cur - 1)
    imp = jnp.where(forced, BIG, imp)
    imp = jnp.where(blk <= cur, imp, NEG)
    rank = jnp.zeros((n_slc, tq), jnp.int32)
    for k in range(n_slc):
        other = imp[k:k + 1, :]
        ahead = (other > imp) | ((other == imp) & (blk > k))
        rank = rank + ahead.astype(jnp.int32)
    bias_ref[...] = jnp.where((rank < SLC_TOPK) & (blk <= cur), 0.0, NEG)

    init = (jnp.full((1, nq), NEG, F32), jnp.zeros((1, nq), F32), jnp.zeros((HEAD_DIM, nq), F32))
    blocks_per_tile = tk // SLC_BLOCK
    tiles_per_step = tk // ATT_TILE
    diag0 = i * (tq // tk)
    band0 = jnp.maximum(diag0 - WIN // tk, 0)
    slab = LANES

    def keys(ref, kt):
        return ref[pl.ds(pl.multiple_of(kt * tk, tk), tk), :]

    def values_t(ref, kt):
        return jnp.concatenate([ref[tiles_per_step * kt + j] for j in range(tiles_per_step)], axis=1)

    def block_bias(kt):
        rows = [jnp.broadcast_to(bias_ref[pl.ds(blocks_per_tile * kt + j, 1), :], (SLC_BLOCK, tq))
                for j in range(blocks_per_tile)]
        return jnp.concatenate(rows, axis=0)

    def far_body(kt, c):
        bias = block_bias(kt)
        slc = (keys(ks_ref, kt), values_t(vst_ref, kt), lambda z, lo: z + bias[:, lo:lo + slab])
        return _attn_update(c, qt, [slc], tq, slab)

    def near_body(kt, c):
        bias = block_bias(kt)
        in_band = rel < WIN - (i * tq - kt * tk)
        slc = (keys(ks_ref, kt), values_t(vst_ref, kt), lambda z, lo: z + bias[:, lo:lo + slab])
        win = (keys(kw_ref, kt), values_t(vwt_ref, kt),
               lambda z, lo: jnp.where(in_band[:, lo:lo + slab], z, NEG))
        return _attn_update(c, qt, [slc, win], tq, slab)

    c_s = lax.fori_loop(0, band0, far_body, init)
    both = tuple(jnp.concatenate([s, w], axis=1) for s, w in zip(c_s, init))
    both = lax.fori_loop(band0, diag0, near_body, both)
    for j in range(tq // tk):
        kt = diag0 + j
        bias = block_bias(kt)
        causal = rel >= j * tk
        slc = (keys(ks_ref, kt), values_t(vst_ref, kt),
               lambda z, lo: jnp.where(causal[:, lo:lo + slab], z + bias[:, lo:lo + slab], NEG))
        win = (keys(kw_ref, kt), values_t(vwt_ref, kt),
               lambda z, lo: jnp.where(causal[:, lo:lo + slab], z, NEG))
        both = _attn_update(both, qt, [slc, win], tq, slab, first_query=j * tk)
    _, l_sw, a_sw = both
    o_sw = a_sw * (1.0 / l_sw)
    o_s = o_sw[:, :nq]
    o_w = o_sw[:, nq:]

    gt = gate_ref[...].T

    def gate(k):
        return jnp.concatenate([gt[r * N_GATES + k:r * N_GATES + k + 1, :] for r in range(GQA_REP)],
                               axis=1)

    o_t = gate(0) * o_c + gate(1) * o_s + gate(2) * o_w
    for r in range(GQA_REP):
        o_ref[:, r * HEAD_DIM:(r + 1) * HEAD_DIM] = o_t[:, r * tq:(r + 1) * tq].T.astype(o_ref.dtype)


def _nsa_attention(qk, vt, cmp, cvg, ovt, batch, seq):
    n = qk.shape[0]
    tq = ATT_Q_TILE
    assert tq % ATT_K_STEP == 0 and tq <= WIN and WIN % ATT_K_STEP == 0
    assert ATT_K_STEP % ATT_TILE == 0 and ATT_TILE % SLC_BLOCK == 0
    qt = seq // tq
    gw = GQA_REP * HEAD_DIM
    k0 = NSA_Q_DIM // HEAD_DIM
    gate0 = 2 * N_KV_HEADS
    n_win = seq // CMP_STRIDE

    def k_spec(which):
        return pl.BlockSpec((seq, HEAD_DIM), lambda b, g, i: (b, k0 + which * N_KV_HEADS + g))

    def vt_spec(which):
        return pl.BlockSpec((None, seq // ATT_TILE, None, HEAD_DIM, ATT_TILE),
                            lambda b, g, i: (which, b, g, 0, 0))

    def cmp_spec(which):
        return pl.BlockSpec((None, None, None, n_win, HEAD_DIM), lambda b, g, i: (b, which, g, 0, 0))

    return pl.pallas_call(
        functools.partial(_nsa_attn_kernel, n_cmp=n_win - CMP_BLOCK // CMP_STRIDE + 1),
        grid=(batch, N_KV_HEADS, qt),
        in_specs=[pl.BlockSpec((tq, gw), lambda b, g, i: (b * qt + i, g)),
                  k_spec(0), k_spec(1), vt_spec(0), vt_spec(1),
                  cmp_spec(0), cmp_spec(1),
                  pl.BlockSpec((tq, LANES), lambda b, g, i: (b * qt + i, gate0 + g)),
                  pl.BlockSpec(ovt.shape, lambda b, g, i: (0, 0))],
        out_specs=pl.BlockSpec((tq, gw), lambda b, g, i: (b * qt + i, g)),
        out_shape=jax.ShapeDtypeStruct((n, NSA_Q_DIM), BF16),
        scratch_shapes=[pltpu.VMEM(ovt.shape[:1] + (tq,), F32)],
        compiler_params=_cparams("parallel", "parallel", "arbitrary"),
        name="nsa_attention",
    )(qk, qk, qk, vt, vt, cmp, cmp, cvg, ovt)


def _rope_tables(seq):
    half = HEAD_DIM // 2
    inv = 1.0 / (ROPE_THETA ** (jnp.arange(half, dtype=F32) / half))
    ang = jnp.arange(seq, dtype=F32)[:, None] * inv[None, :]
    cos = jnp.cos(ang)
    sin = jnp.sin(ang)
    return jnp.concatenate([cos, cos], axis=1), jnp.concatenate([-sin, sin], axis=1)


def _overlap_matrix(seq):
    n_win = seq // CMP_STRIDE
    n_cmp = n_win - CMP_BLOCK // CMP_STRIDE + 1
    sj = np.arange(seq // SLC_BLOCK)[:, None]
    ci = np.arange(n_win)[None, :]
    ov = ((ci * CMP_STRIDE <= (sj + 1) * SLC_BLOCK - 1)
          & (ci * CMP_STRIDE + CMP_BLOCK - 1 >= sj * SLC_BLOCK) & (ci < n_cmp))
    return jnp.asarray(ov, BF16)


def _nsa_mixer(u, h, w_in, gate_b, cmp_pe, cmp_w1, cmp_b1, cmp_w2, cmp_b2, w_out, ga, gb, batch, seq):
    assert seq // CMP_STRIDE == ATT_TILE and seq % PROJ_ROW_TILE == 0
    q0 = NSA_Q_DIM
    kvd = NSA_KV_DIM
    tn = kvd
    n_q = q0 // tn
    kc, vc, ks, vs, kw, vw = range(n_q, n_q + 6)
    per_group = GQA_REP * N_GATES
    w_g = w_in[:, q0 + 6 * kvd:].reshape(D_MODEL, N_KV_HEADS, per_group)
    w_g = jnp.pad(w_g, ((0, 0), (0, 0), (0, LANES - per_group))).reshape(D_MODEL, N_KV_HEADS * LANES)
    w_g = w_g.astype(BF16)
    b_g = jnp.pad(gate_b.reshape(N_KV_HEADS, per_group), ((0, 0), (0, LANES - per_group)))
    bias_c = jnp.concatenate([jnp.zeros((2 * kvd,), F32), b_g.reshape(-1)])[None, :]
    cos, sin = _rope_tables(seq)

    q_scale = HEAD_DIM ** -0.5 * math.log2(math.e)
    oscale_qk = jnp.concatenate([jnp.full((q0,), q_scale, F32), jnp.ones((2 * kvd,), F32)])[None, :]
    w_t = w_in.T
    qk = _nsa_proj(u, w_t, w_g, jnp.zeros_like(oscale_qk), oscale_qk, cos, sin, tn=tn,
                   w_chunk=lambda j: jnp.where(j < n_q, j, ks + (kw - ks) * (j - n_q)),
                   out_dtype=BF16, rope_chunks=tuple(range(n_q + 2)), gate_chunk=None, seq=seq)
    vt = _nsa_vt(u, w_t, lambda j: vs + (vw - vs) * j)
    cvg = _nsa_proj(u, w_t, w_g, bias_c, jnp.ones_like(bias_c), cos, sin, tn=tn,
                    w_chunk=lambda j: jnp.minimum(kc + j, vc),
                    out_dtype=F32, rope_chunks=(0,), gate_chunk=2, seq=seq)
    cmp = _compress(cvg, cmp_pe, cmp_w1, cmp_b1[:, None, :], cmp_w2, cmp_b2[:, None, :], batch, seq)
    o = _nsa_attention(qk, vt, cmp, cvg, _overlap_matrix(seq), batch, seq)
    return _out_proj(o, w_out.astype(BF16), h, ga, gb)


def kernel(x, norm_g, ffn_w_up, ffn_conv_w, ffn_conv_b, ffn_w_down, pool_w_in, pool_w_grp, pool_scale,
           pool_w_out, sgu_w_in, sgu_ln_g, sgu_ln_b, sgu_w_s, sgu_b_s, sgu_w_out, nsa_w_in, nsa_gate_b,
           nsa_cmp_pe, nsa_cmp_w1, nsa_cmp_b1, nsa_cmp_w2, nsa_cmp_b2, nsa_w_out):
    batch, seq, d = x.shape
    h = x.reshape(batch * seq, d)

    def gain(i, k):
        return norm_g[i, k][None, :]

    ffn_up, ffn_down = ffn_w_up, ffn_w_down
    pool_in, pool_grp, pool_out = pool_w_in.astype(BF16), pool_w_grp.astype(BF16), pool_w_out.astype(BF16)
    gains = norm_g[:, :, None, :]
    conv_b = ffn_conv_b[:, None, :]

    u = gain(0, 0)
    for i in range(DEPTH):
        kind, j = i % 3, i // 3
        ga, gb = gain(i, 1), gain(i, 2)
        if kind == 0:
            h, u = _pool_mixer(u, h, pool_in, pool_grp, pool_scale[j][None, :], pool_out, ga, gb, j, seq)
        elif kind == 1:
            h, u = _sgu_mixer(u, h, sgu_w_in[j].astype(BF16), sgu_ln_g[j][None, :], sgu_ln_b[j][None, :],
                              sgu_w_s[j], sgu_b_s[j].T, sgu_w_out[j].astype(BF16), ga, gb)
        else:
            h, u = _nsa_mixer(u, h, nsa_w_in[j], nsa_gate_b[j], nsa_cmp_pe[j], nsa_cmp_w1[j],
                              nsa_cmp_b1[j], nsa_cmp_w2[j], nsa_cmp_b2[j], nsa_w_out[j], ga, gb,
                              batch, seq)
        next_gain = (i + 1, 0) if i + 1 < DEPTH else (i, 3)
        h, u = _conv_ffn(u, h, ffn_up, ffn_conv_w, conv_b, ffn_down, gains, i, next_gain, seq)
    return h.reshape(batch, seq, d)
```

```python
import functools
import math

import jax
import jax.numpy as jnp
import numpy as np
from jax import lax
from jax.experimental import pallas as pl
from jax.experimental.pallas import tpu as pltpu

F32 = jnp.float32
BF16 = jnp.bfloat16

D_MODEL = 2048
DEPTH = 4
RMS_EPS = 1e-6
LN_EPS = 1e-5
NEG = -1e30
BIG = 1e30

FFN_DIM = 5632
CONV_WIDTH = 3
POOL_WINDOWS = (2, 4, 8, 16)
POOL_GROUP_DIM = D_MODEL // len(POOL_WINDOWS)
SGU_CHUNK = 128
SGU_GROUPS = 16
SGU_GROUP_DIM = D_MODEL // SGU_GROUPS

HEAD_DIM = 128
N_HEADS = 16
N_KV_HEADS = 4
GQA_REP = N_HEADS // N_KV_HEADS
ROPE_THETA = 10000.0
CMP_BLOCK = 32
CMP_STRIDE = 16
CMP_HIDDEN = 2 * HEAD_DIM
SLC_BLOCK = 64
SLC_TOPK = 16
WIN = 512
NSA_Q_DIM = N_HEADS * HEAD_DIM
NSA_KV_DIM = N_KV_HEADS * HEAD_DIM
N_GATES = 3

LANES = 128
SUBLANES = 8
BF16_SUBLANES = 16
VMEM_LIMIT = 56 * 1024 * 1024
FFN_VMEM_LIMIT = 60 * 1024 * 1024

ROW_TILE = 512
PROJ_ROW_TILE = 2048
MIX_ROW_TILE = 256
FFN_ROW_TILE = 1024
FFN_COL_TILE = 512
ATT_TILE = 128
ATT_K_STEP = 512
ATT_Q_TILE = 512


def _cparams(*sem):
    return pltpu.CompilerParams(dimension_semantics=sem, vmem_limit_bytes=VMEM_LIMIT)


def _const_spec(shape):
    nd = len(shape)
    return pl.BlockSpec(shape, lambda *_: (0,) * nd, pipeline_mode=pl.Buffered(1))


def _layer_spec(stacked_shape, layer):
    nd = len(stacked_shape) - 1
    return pl.BlockSpec((None,) + tuple(stacked_shape[1:]), lambda *_: (layer,) + (0,) * nd,
                        pipeline_mode=pl.Buffered(1))


def _rms(x):
    return x * lax.rsqrt(jnp.mean(x * x, axis=-1, keepdims=True) + RMS_EPS)


def _residual_update(m, h_ref, ga_ref, gb_ref, ho_ref, uo_ref):
    hn = h_ref[...] + _rms(m) * ga_ref[...]
    ho_ref[...] = hn
    uo_ref[...] = (_rms(hn) * gb_ref[...]).astype(BF16)


def _dot(a, b):
    return jnp.dot(a, b, preferred_element_type=F32)


def _dot_nt(a, b):
    return lax.dot_general(a, b, (((1,), (1,)), ((), ())), preferred_element_type=F32)


def _out_proj_kernel(x_ref, w_ref, h_ref, ga_ref, gb_ref, ho_ref, uo_ref):
    _residual_update(_dot(x_ref[...], w_ref[...]), h_ref, ga_ref, gb_ref, ho_ref, uo_ref)


def _out_proj(xin, w, h, ga, gb):
    n, k = xin.shape
    tm = ROW_TILE
    row = lambda i: (i, 0)
    return pl.pallas_call(
        _out_proj_kernel,
        grid=(n // tm,),
        in_specs=[pl.BlockSpec((tm, k), row), _const_spec((k, D_MODEL)),
                  pl.BlockSpec((tm, D_MODEL), row),
                  pl.BlockSpec((1, D_MODEL), lambda i: (0, 0)),
                  pl.BlockSpec((1, D_MODEL), lambda i: (0, 0))],
        out_specs=[pl.BlockSpec((tm, D_MODEL), row), pl.BlockSpec((tm, D_MODEL), row)],
        out_shape=[jax.ShapeDtypeStruct((n, D_MODEL), F32),
                   jax.ShapeDtypeStruct((n, D_MODEL), BF16)],
        compiler_params=_cparams("parallel"),
        name="out_proj",
    )(xin, w, h, ga, gb)


def _ffn_kernel(u_ref, wa_ref, wb_ref, cw_ref, cb_ref, wd_ref, h_hbm, ga_ref, gb_ref,
                ho_hbm, uo_hbm, tail_ref, acc_ref, hbuf_ref, ubuf_ref, sem_ref, *, seq_tiles):
    i = pl.program_id(0)
    c = pl.program_id(1)
    n_i = pl.num_programs(0)
    n_c = pl.num_programs(1)
    tm = u_ref.shape[0]
    keep = tail_ref.shape[1]

    def rows(tile):
        return pl.ds(pl.multiple_of(tile * tm, tm), tm)

    def h_load():
        return pltpu.make_async_copy(h_hbm.at[rows(i)], hbuf_ref, sem_ref.at[0])

    def h_store(tile):
        return pltpu.make_async_copy(hbuf_ref, ho_hbm.at[rows(tile)], sem_ref.at[1])

    def u_store(tile):
        return pltpu.make_async_copy(ubuf_ref, uo_hbm.at[rows(tile)], sem_ref.at[2])

    @pl.when(c == 1)
    def _():
        @pl.when(i > 0)
        def _():
            h_store(i - 1).wait()
            u_store(i - 1).wait()
        h_load().start()

    @pl.when(i % seq_tiles == 0)
    def _():
        tail_ref[c] = jnp.zeros(tail_ref.shape[1:], F32)

    @pl.when(c == 0)
    def _():
        acc_ref[...] = jnp.zeros_like(acc_ref)

    u = u_ref[...]
    a = _dot(u, wa_ref[...].astype(BF16))
    b = _dot(u, wb_ref[...].astype(BF16))
    prev = tail_ref[c]
    tail_ref[c] = a[tm - keep:tm]
    row = lax.broadcasted_iota(jnp.int32, a.shape, 0)
    last1 = prev[keep - 1:keep]
    last2 = prev[keep - 2:keep - 1]
    p1 = jnp.where(row == 0, last1, pltpu.roll(a, 1, 0))
    p2 = jnp.where(row == 0, last2, jnp.where(row == 1, last1, pltpu.roll(a, 2, 0)))
    cw = cw_ref[...]
    y = cw[0:1] * p2 + cw[1:2] * p1 + cw[2:3] * a + cb_ref[...]
    gated = y / (1.0 + jnp.exp(-y)) * b
    acc_ref[...] += _dot(gated.astype(BF16), wd_ref[...].astype(BF16))

    @pl.when(c == n_c - 1)
    def _():
        h_load().wait()
        _residual_update(acc_ref[...], hbuf_ref, ga_ref, gb_ref, hbuf_ref, ubuf_ref)
        h_store(i).start()
        u_store(i).start()

        @pl.when(i == n_i - 1)
        def _():
            h_store(i).wait()
            u_store(i).wait()


def _conv_ffn(u, h, w_up, conv_w, conv_b, w_down, norm_g, layer, next_gain, seq):
    n = u.shape[0]
    tm, tf = FFN_ROW_TILE, FFN_COL_TILE
    n_chunks = FFN_DIM // tf
    assert n_chunks >= 2

    def gain_spec(lyr, slot):
        return pl.BlockSpec((None, None, 1, D_MODEL), lambda i, c: (lyr, slot, 0, 0))

    return pl.pallas_call(
        functools.partial(_ffn_kernel, seq_tiles=seq // tm),
        grid=(n // tm, n_chunks),
        in_specs=[pl.BlockSpec((tm, D_MODEL), lambda i, c: (i, 0)),
                  pl.BlockSpec((None, D_MODEL, tf), lambda i, c: (layer, 0, c)),
                  pl.BlockSpec((None, D_MODEL, tf), lambda i, c: (layer, 0, c + n_chunks)),
                  pl.BlockSpec((None, CONV_WIDTH, tf), lambda i, c: (layer, 0, c)),
                  pl.BlockSpec((None, 1, tf), lambda i, c: (layer, 0, c)),
                  pl.BlockSpec((None, tf, D_MODEL), lambda i, c: (layer, c, 0)),
                  pl.BlockSpec(memory_space=pl.ANY),
                  gain_spec(layer, 3),
                  gain_spec(*next_gain)],
        out_specs=[pl.BlockSpec(memory_space=pl.ANY), pl.BlockSpec(memory_space=pl.ANY)],
        out_shape=[jax.ShapeDtypeStruct((n, D_MODEL), F32),
                   jax.ShapeDtypeStruct((n, D_MODEL), BF16)],
        scratch_shapes=[pltpu.VMEM((n_chunks, SUBLANES, tf), F32),
                        pltpu.VMEM((tm, D_MODEL), F32),
                        pltpu.VMEM((tm, D_MODEL), F32),
                        pltpu.VMEM((tm, D_MODEL), BF16),
                        pltpu.SemaphoreType.DMA((3,))],
        compiler_params=pltpu.CompilerParams(dimension_semantics=("arbitrary", "arbitrary"),
                                             vmem_limit_bytes=FFN_VMEM_LIMIT),
        name="conv_ffn",
    )(u, w_up, w_up, conv_w, conv_b, w_down, h, norm_g, norm_g)


def _pool_kernel(u_ref, win_ref, wgrp_ref, scale_ref, wout_ref, h_ref, ga_ref, gb_ref,
                 ho_ref, uo_ref, tail_ref, m_ref, *, seq_tiles, norm_input):
    i = pl.program_id(0)
    tm = h_ref.shape[0]
    halo = tail_ref.shape[0]
    assert halo >= max(POOL_WINDOWS) - 1

    @pl.when(i % seq_tiles == 0)
    def _():
        tail_ref[...] = jnp.zeros_like(tail_ref)

    u = (_rms(h_ref[...]) * u_ref[...]).astype(BF16) if norm_input else u_ref[...]
    tpos = (i % seq_tiles) * tm + lax.broadcasted_iota(jnp.int32, (tm, 1), 0)
    for g, w in enumerate(POOL_WINDOWS):
        cols = slice(g * POOL_GROUP_DIM, (g + 1) * POOL_GROUP_DIM)
        z = _dot(u, win_ref[:, cols])
        x = jnp.concatenate([tail_ref[:, cols], z], axis=0)
        tail_ref[:, cols] = z[tm - halo:]
        s = x
        k = 1
        while k < w:
            s = s + pltpu.roll(s, k, 0)
            k *= 2
        cnt = jnp.minimum(tpos + 1, w).astype(F32)
        p = s[halo:] / cnt - z
        mg = _dot(p.astype(BF16), wgrp_ref[g]) * scale_ref[:, cols]
        m_ref[:, cols] = mg.astype(BF16)
    _residual_update(_dot(m_ref[...], wout_ref[...]), h_ref, ga_ref, gb_ref, ho_ref, uo_ref)


def _pool_mixer(u, h, w_in, w_grp, scale, w_out, ga, gb, layer, seq):
    n = h.shape[0]
    tm = ROW_TILE
    row = lambda i: (i, 0)
    norm_input = u.shape[0] == 1
    u_spec = _const_spec((1, D_MODEL)) if norm_input else pl.BlockSpec((tm, D_MODEL), row)
    return pl.pallas_call(
        functools.partial(_pool_kernel, seq_tiles=seq // tm, norm_input=norm_input),
        grid=(n // tm,),
        in_specs=[u_spec,
                  _layer_spec(w_in.shape, layer),
                  _layer_spec(w_grp.shape, layer),
                  _const_spec((1, D_MODEL)),
                  _layer_spec(w_out.shape, layer),
                  pl.BlockSpec((tm, D_MODEL), row),
                  _const_spec((1, D_MODEL)),
                  _const_spec((1, D_MODEL))],
        out_specs=[pl.BlockSpec((tm, D_MODEL), row), pl.BlockSpec((tm, D_MODEL), row)],
        out_shape=[jax.ShapeDtypeStruct((n, D_MODEL), F32),
                   jax.ShapeDtypeStruct((n, D_MODEL), BF16)],
        scratch_shapes=[pltpu.VMEM((BF16_SUBLANES, D_MODEL), F32), pltpu.VMEM((tm, D_MODEL), BF16)],
        compiler_params=_cparams("arbitrary"),
        name="pool_mixer",
    )(u, w_in, w_grp, scale, w_out, h, ga, gb)


def _gelu_tanh(x):
    c = math.sqrt(2.0 / math.pi)
    return 0.5 * x * (1.0 + jnp.tanh(c * (x + 0.044715 * (x * x * x))))


def _sgu_kernel(u_ref, win_ref, lng_ref, lnb_ref, ws_ref, bst_ref, wout_ref, h_ref, ga_ref, gb_ref,
                ho_ref, uo_ref, gated_ref):
    tm = u_ref.shape[0]
    y = _gelu_tanh(_dot(u_ref[...], win_ref[...]))
    uu = y[:, :D_MODEL]
    v = y[:, D_MODEL:]
    mu = jnp.mean(v, axis=-1, keepdims=True)
    vc = v - mu
    var = jnp.mean(vc * vc, axis=-1, keepdims=True)
    vn = (vc * lax.rsqrt(var + LN_EPS) * lng_ref[...] + lnb_ref[...]).astype(BF16)
    t_idx = lax.broadcasted_iota(jnp.int32, (SGU_CHUNK, SGU_CHUNK), 0)
    s_idx = lax.broadcasted_iota(jnp.int32, (SGU_CHUNK, SGU_CHUNK), 1)
    causal = s_idx <= t_idx
    bst = bst_ref[...]
    for g in range(SGU_GROUPS):
        cols = slice(g * SGU_GROUP_DIM, (g + 1) * SGU_GROUP_DIM)
        ws = jnp.where(causal, ws_ref[g], 0.0).astype(BF16)
        bias = bst[:, g:g + 1]
        for ci in range(tm // SGU_CHUNK):
            rows = slice(ci * SGU_CHUNK, (ci + 1) * SGU_CHUNK)
            mixed = _dot(ws, vn[rows, cols]) + bias
            gated_ref[rows, cols] = (uu[rows, cols] * mixed).astype(BF16)
    _residual_update(_dot(gated_ref[...], wout_ref[...]), h_ref, ga_ref, gb_ref, ho_ref, uo_ref)


def _sgu_mixer(u, h, w_in, ln_g, ln_b, w_s, b_s_t, w_out, ga, gb):
    n = u.shape[0]
    tm = MIX_ROW_TILE
    row = lambda i: (i, 0)
    return pl.pallas_call(
        _sgu_kernel,
        grid=(n // tm,),
        in_specs=[pl.BlockSpec((tm, D_MODEL), row),
                  _const_spec((D_MODEL, 2 * D_MODEL)),
                  _const_spec((1, D_MODEL)),
                  _const_spec((1, D_MODEL)),
                  _const_spec(w_s.shape),
                  _const_spec(b_s_t.shape),
                  _const_spec((D_MODEL, D_MODEL)),
                  pl.BlockSpec((tm, D_MODEL), row),
                  _const_spec((1, D_MODEL)),
                  _const_spec((1, D_MODEL))],
        out_specs=[pl.BlockSpec((tm, D_MODEL), row), pl.BlockSpec((tm, D_MODEL), row)],
        out_shape=[jax.ShapeDtypeStruct((n, D_MODEL), F32),
                   jax.ShapeDtypeStruct((n, D_MODEL), BF16)],
        scratch_shapes=[pltpu.VMEM((tm, D_MODEL), BF16)],
        compiler_params=_cparams("parallel"),
        name="sgu_mixer",
    )(u, w_in, ln_g, ln_b, w_s, b_s_t, w_out, h, ga, gb)


def _nsa_proj_kernel(u_ref, w_ref, wg_ref, bias_ref, oscale_ref, cos_ref, sin_ref, o_ref, *,
                     rope_chunks, gate_chunk):
    j = pl.program_id(1)
    tn = o_ref.shape[1]

    def is_in(chunks):
        hit = j == chunks[0]
        for c in chunks[1:]:
            hit = hit | (j == c)
        return hit

    def project():
        return _dot_nt(u_ref[...], w_ref[...].astype(BF16))

    rope = is_in(rope_chunks)
    plain = jnp.logical_not(rope)
    if gate_chunk is not None:
        plain = plain & (j != gate_chunk)

        @pl.when(j == gate_chunk)
        def _():
            acc = _dot(u_ref[...], wg_ref[...])
            o_ref[...] = (1.0 / (1.0 + jnp.exp(-(acc + bias_ref[...])))).astype(o_ref.dtype)

    @pl.when(rope)
    def _():
        acc = project()
        cos = cos_ref[...]
        sin = sin_ref[...]
        for hh in range(tn // HEAD_DIM):
            cols = slice(hh * HEAD_DIM, (hh + 1) * HEAD_DIM)
            x = acc[:, cols]
            roped = x * cos + pltpu.roll(x, HEAD_DIM // 2, 1) * sin
            o_ref[:, cols] = (roped * oscale_ref[:, cols]).astype(o_ref.dtype)

    @pl.when(plain)
    def _():
        o_ref[...] = project().astype(o_ref.dtype)


def _nsa_proj(u, w_in, w_gate, bias, oscale, cos, sin, *, tn, w_chunk, out_dtype, rope_chunks,
              gate_chunk, seq):
    n = u.shape[0]
    ncols = bias.shape[1]
    tm = PROJ_ROW_TILE
    seq_tiles = seq // tm
    return pl.pallas_call(
        functools.partial(_nsa_proj_kernel, rope_chunks=rope_chunks, gate_chunk=gate_chunk),
        grid=(n // tm, ncols // tn),
        in_specs=[pl.BlockSpec((tm, D_MODEL), lambda i, j: (i, 0)),
                  pl.BlockSpec((tn, D_MODEL), lambda i, j: (w_chunk(j), 0)),
                  pl.BlockSpec(w_gate.shape, lambda i, j: (0, 0)),
                  pl.BlockSpec((1, tn), lambda i, j: (0, j)),
                  pl.BlockSpec((1, tn), lambda i, j: (0, j)),
                  pl.BlockSpec((tm, HEAD_DIM), lambda i, j: (i % seq_tiles, 0)),
                  pl.BlockSpec((tm, HEAD_DIM), lambda i, j: (i % seq_tiles, 0))],
        out_specs=pl.BlockSpec((tm, tn), lambda i, j: (i, j)),
        out_shape=jax.ShapeDtypeStruct((n, ncols), out_dtype),
        compiler_params=_cparams("parallel", "arbitrary"),
        name="nsa_proj",
    )(u, w_in, w_gate, bias, oscale, cos, sin)


def _nsa_vt_kernel(u_ref, w_ref, o_ref):
    acc = _dot_nt(u_ref[...], w_ref[...].astype(BF16))
    for kt in range(acc.shape[0] // ATT_TILE):
        for g in range(N_KV_HEADS):
            tile = acc[kt * ATT_TILE:(kt + 1) * ATT_TILE, g * HEAD_DIM:(g + 1) * HEAD_DIM]
            o_ref[kt, g] = tile.T.astype(o_ref.dtype)


def _nsa_vt(u, w_in, w_chunk):
    n = u.shape[0]
    tm = PROJ_ROW_TILE
    kt = tm // ATT_TILE
    return pl.pallas_call(
        _nsa_vt_kernel,
        grid=(n // tm, 2),
        in_specs=[pl.BlockSpec((tm, D_MODEL), lambda i, j: (i, 0)),
                  pl.BlockSpec((NSA_KV_DIM, D_MODEL), lambda i, j: (w_chunk(j), 0))],
        out_specs=pl.BlockSpec((None, kt, N_KV_HEADS, HEAD_DIM, ATT_TILE), lambda i, j: (j, i, 0, 0, 0)),
        out_shape=jax.ShapeDtypeStruct((2, n // ATT_TILE, N_KV_HEADS, HEAD_DIM, ATT_TILE), BF16),
        compiler_params=_cparams("parallel", "arbitrary"),
        name="nsa_vt",
    )(u, w_in)


def _compress_kernel(a_ref, pe_ref, w1_ref, b1_ref, w2_ref, b2_ref, o_ref):
    nb = a_ref.shape[0] // CMP_STRIDE
    top = jnp.zeros((nb, CMP_HIDDEN), F32)
    bot = jnp.zeros((nb, CMP_HIDDEN), F32)
    for p in range(CMP_STRIDE):
        xp = a_ref[pl.ds(p, nb, stride=CMP_STRIDE), :]
        lo = slice(p * HEAD_DIM, (p + 1) * HEAD_DIM)
        hi = slice((CMP_STRIDE + p) * HEAD_DIM, (CMP_STRIDE + p + 1) * HEAD_DIM)
        top += _dot((xp + pe_ref[p:p + 1, :]).astype(BF16), w1_ref[lo, :].astype(BF16))
        bot += _dot((xp + pe_ref[CMP_STRIDE + p:CMP_STRIDE + p + 1, :]).astype(BF16),
                    w1_ref[hi, :].astype(BF16))
    hid = _gelu_tanh(top + pltpu.roll(bot, nb - 1, 0) + b1_ref[...])
    res = _dot(hid.astype(BF16), w2_ref[...].astype(BF16)) + b2_ref[...]

    @pl.when(pl.program_id(0) == 0)
    def _():
        o_ref[...] = res.astype(o_ref.dtype)

    @pl.when(pl.program_id(0) == 1)
    def _():
        o_ref[...] = res.T.astype(o_ref.dtype)


def _compress(cv, pe, w1, b1, w2, b2, batch, seq):
    nb = seq // CMP_STRIDE
    return pl.pallas_call(
        _compress_kernel,
        grid=(2, batch, N_KV_HEADS),
        in_specs=[pl.BlockSpec((seq, HEAD_DIM), lambda s, b, g: (b, s * N_KV_HEADS + g)),
                  pl.BlockSpec((None, CMP_BLOCK, HEAD_DIM), lambda s, b, g: (s, 0, 0)),
                  pl.BlockSpec((None, CMP_BLOCK * HEAD_DIM, CMP_HIDDEN), lambda s, b, g: (s, 0, 0)),
                  pl.BlockSpec((None, 1, CMP_HIDDEN), lambda s, b, g: (s, 0, 0)),
                  pl.BlockSpec((None, CMP_HIDDEN, HEAD_DIM), lambda s, b, g: (s, 0, 0)),
                  pl.BlockSpec((None, 1, HEAD_DIM), lambda s, b, g: (s, 0, 0))],
        out_specs=pl.BlockSpec((None, None, None, nb, HEAD_DIM), lambda s, b, g: (b, s, g, 0, 0)),
        out_shape=jax.ShapeDtypeStruct((batch, 2, N_KV_HEADS, nb, HEAD_DIM), BF16),
        compiler_params=_cparams("parallel", "parallel", "parallel"),
        name="nsa_compress",
    )(cv, pe, w1, b1, w2, b2)


def _attn_update(carry, qt, branches, tq, slab, first_query=0):
    m, l, acc = carry
    ms, ls, accs = [], [], []
    for b, (k, vt, fix) in enumerate(branches):
        for r in range(GQA_REP):
            for lo in range(0, tq, slab):
                q0 = r * tq + lo
                c0 = b * GQA_REP * tq + q0
                if lo + slab <= first_query:
                    ms.append(m[:, c0:c0 + slab])
                    ls.append(l[:, c0:c0 + slab])
                    accs.append(acc[:, c0:c0 + slab])
                    continue
                z = fix(_dot(k, qt[:, q0:q0 + slab]), lo)
                m_old = m[:, c0:c0 + slab]
                m_new = jnp.maximum(m_old, jnp.max(z, axis=0, keepdims=True))
                alpha = jnp.exp2(m_old - m_new)
                p = jnp.exp2(z - m_new)
                ms.append(m_new)
                ls.append(alpha * l[:, c0:c0 + slab] + jnp.sum(p, axis=0, keepdims=True))
                accs.append(alpha * acc[:, c0:c0 + slab] + _dot(vt, p.astype(BF16)))
    return tuple(jnp.concatenate(x, axis=1) for x in (ms, ls, accs))


def _nsa_attn_kernel(q_ref, ks_ref, kw_ref, vst_ref, vwt_ref, kc_ref, vct_ref, gate_ref, ovt_ref,
                     o_ref, bias_ref, *, n_cmp):
    i = pl.program_id(2)
    tq = q_ref.shape[0]
    tk = ATT_K_STEP
    nq = GQA_REP * tq
    n_slc, n_win = ovt_ref.shape

    q = q_ref[...].astype(F32)
    qt = jnp.concatenate([q[:, r * HEAD_DIM:(r + 1) * HEAD_DIM].T for r in range(GQA_REP)],
                         axis=1).astype(BF16)
    rel = lax.broadcasted_iota(jnp.int32, (tk, tq), 1) - lax.broadcasted_iota(jnp.int32, (tk, tq), 0)

    win_idx = lax.broadcasted_iota(jnp.int32, (n_win, nq), 0)
    t_abs = i * tq + (lax.broadcasted_iota(jnp.int32, (n_win, nq), 1) & (tq - 1))
    sc = _dot(kc_ref[...], qt)
    ok_c = (win_idx * CMP_STRIDE + (CMP_BLOCK - 1) <= t_abs) & (win_idx < n_cmp)
    sc = jnp.where(ok_c, sc, NEG)
    e = jnp.exp2(sc - jnp.max(sc, axis=0, keepdims=True))
    p_c = e * (1.0 / jnp.sum(e, axis=0, keepdims=True))
    p_c = jnp.where(t_abs >= CMP_BLOCK - 1, p_c, 0.0)
    o_c = _dot(vct_ref[...], p_c.astype(BF16))

    p_sum = p_c[:, 0:tq]
    for r in range(1, GQA_REP):
        p_sum = p_sum + p_c[:, r * tq:(r + 1) * tq]
    p_hi = p_sum.astype(BF16)
    p_lo = (p_sum - p_hi.astype(F32)).astype(BF16)
    ovt = ovt_ref[...]
    imp = _dot(ovt, p_hi) + _dot(ovt, p_lo)
    blk = lax.broadcasted_iota(jnp.int32, (n_slc, tq), 0)
    cur = (i * tq + lax.broadcasted_iota(jnp.int32, (n_slc, tq), 1)) >> int(math.log2(SLC_BLOCK))
    forced = (blk == 0) | (blk == cur) | (blk == cur - 1)
    imp = jnp.where(forced, BIG, imp)
    imp = jnp.where(blk <= cur, imp, NEG)
    rank = jnp.zeros((n_slc, tq), jnp.int32)
    for k in range(n_slc):
        other = imp[k:k + 1, :]
        ahead = (other > imp) | ((other == imp) & (blk > k))
        rank = rank + ahead.astype(jnp.int32)
    bias_ref[...] = jnp.where((rank < SLC_TOPK) & (blk <= cur), 0.0, NEG)

    init = (jnp.full((1, nq), NEG, F32), jnp.zeros((1, nq), F32), jnp.zeros((HEAD_DIM, nq), F32))
    blocks_per_tile = tk // SLC_BLOCK
    tiles_per_step = tk // ATT_TILE
    diag0 = i * (tq // tk)
    band0 = jnp.maximum(diag0 - WIN // tk, 0)
    slab = LANES

    def keys(ref, kt):
        return ref[pl.ds(pl.multiple_of(kt * tk, tk), tk), :]

    def values_t(ref, kt):
        return jnp.concatenate([ref[tiles_per_step * kt + j] for j in range(tiles_per_step)], axis=1)

    def block_bias(kt):
        rows = [jnp.broadcast_to(bias_ref[pl.ds(blocks_per_tile * kt + j, 1), :], (SLC_BLOCK, tq))
                for j in range(blocks_per_tile)]
        return jnp.concatenate(rows, axis=0)

    def far_body(kt, c):
        bias = block_bias(kt)
        slc = (keys(ks_ref, kt), values_t(vst_ref, kt), lambda z, lo: z + bias[:, lo:lo + slab])
        return _attn_update(c, qt, [slc], tq, slab)

    def near_body(kt, c):
        bias = block_bias(kt)
        in_band = rel < WIN - (i * tq - kt * tk)
        slc = (keys(ks_ref, kt), values_t(vst_ref, kt), lambda z, lo: z + bias[:, lo:lo + slab])
        win = (keys(kw_ref, kt), values_t(vwt_ref, kt),
               lambda z, lo: jnp.where(in_band[:, lo:lo + slab], z, NEG))
        return _attn_update(c, qt, [slc, win], tq, slab)

    c_s = lax.fori_loop(0, band0, far_body, init)
    both = tuple(jnp.concatenate([s, w], axis=1) for s, w in zip(c_s, init))
    both = lax.fori_loop(band0, diag0, near_body, both)
    for j in range(tq // tk):
        kt = diag0 + j
        bias = block_bias(kt)
        causal = rel >= j * tk
        slc = (keys(ks_ref, kt), values_t(vst_ref, kt),
               lambda z, lo: jnp.where(causal[:, lo:lo + slab], z + bias[:, lo:lo + slab], NEG))
        win = (keys(kw_ref, kt), values_t(vwt_ref, kt),
               lambda z, lo: jnp.where(causal[:, lo:lo + slab], z, NEG))
        both = _attn_update(both, qt, [slc, win], tq, slab, first_query=j * tk)
    _, l_sw, a_sw = both
    o_sw = a_sw * (1.0 / l_sw)
    o_s = o_sw[:, :nq]
    o_w = o_sw[:, nq:]

    gt = gate_ref[...].T

    def gate(k):
        return jnp.concatenate([gt[r * N_GATES + k:r * N_GATES + k + 1, :] for r in range(GQA_REP)],
                               axis=1)

    o_t = gate(0) * o_c + gate(1) * o_s + gate(2) * o_w
    for r in range(GQA_REP):
        o_ref[:, r * HEAD_DIM:(r + 1) * HEAD_DIM] = o_t[:, r * tq:(r + 1) * tq].T.astype(o_ref.dtype)


def _nsa_attention(qk, vt, cmp, cvg, ovt, batch, seq):
    n = qk.shape[0]
    tq = ATT_Q_TILE
    assert tq % ATT_K_STEP == 0 and tq <= WIN and WIN % ATT_K_STEP == 0
    assert ATT_K_STEP % ATT_TILE == 0 and ATT_TILE % SLC_BLOCK == 0
    qt = seq // tq
    gw = GQA_REP * HEAD_DIM
    k0 = NSA_Q_DIM // HEAD_DIM
    gate0 = 2 * N_KV_HEADS
    n_win = seq // CMP_STRIDE

    def k_spec(which):
        return pl.BlockSpec((seq, HEAD_DIM), lambda b, g, i: (b, k0 + which * N_KV_HEADS + g))

    def vt_spec(which):
        return pl.BlockSpec((None, seq // ATT_TILE, None, HEAD_DIM, ATT_TILE),
                            lambda b, g, i: (which, b, g, 0, 0))

    def cmp_spec(which):
        return pl.BlockSpec((None, None, None, n_win, HEAD_DIM), lambda b, g, i: (b, which, g, 0, 0))

    return pl.pallas_call(
        functools.partial(_nsa_attn_kernel, n_cmp=n_win - CMP_BLOCK // CMP_STRIDE + 1),
        grid=(batch, N_KV_HEADS, qt),
        in_specs=[pl.BlockSpec((tq, gw), lambda b, g, i: (b * qt + i, g)),
                  k_spec(0), k_spec(1), vt_spec(0), vt_spec(1),
                  cmp_spec(0), cmp_spec(1),
                  pl.BlockSpec((tq, LANES), lambda b, g, i: (b * qt + i, gate0 + g)),
                  pl.BlockSpec(ovt.shape, lambda b, g, i: (0, 0))],
        out_specs=pl.BlockSpec((tq, gw), lambda b, g, i: (b * qt + i, g)),
        out_shape=jax.ShapeDtypeStruct((n, NSA_Q_DIM), BF16),
        scratch_shapes=[pltpu.VMEM(ovt.shape[:1] + (tq,), F32)],
        compiler_params=_cparams("parallel", "parallel", "arbitrary"),
        name="nsa_attention",
    )(qk, qk, qk, vt, vt, cmp, cmp, cvg, ovt)


def _rope_tables(seq):
    half = HEAD_DIM // 2
    inv = 1.0 / (ROPE_THETA ** (jnp.arange(half, dtype=F32) / half))
    ang = jnp.arange(seq, dtype=F32)[:, None] * inv[None, :]
    cos = jnp.cos(ang)
    sin = jnp.sin(ang)
    return jnp.concatenate([cos, cos], axis=1), jnp.concatenate([-sin, sin], axis=1)


def _overlap_matrix(seq):
    n_win = seq // CMP_STRIDE
    n_cmp = n_win - CMP_BLOCK // CMP_STRIDE + 1
    sj = np.arange(seq // SLC_BLOCK)[:, None]
    ci = np.arange(n_win)[None, :]
    ov = ((ci * CMP_STRIDE <= (sj + 1) * SLC_BLOCK - 1)
          & (ci * CMP_STRIDE + CMP_BLOCK - 1 >= sj * SLC_BLOCK) & (ci < n_cmp))
    return jnp.asarray(ov, BF16)


def _nsa_mixer(u, h, w_in, gate_b, cmp_pe, cmp_w1, cmp_b1, cmp_w2, cmp_b2, w_out, ga, gb, batch, seq):
    assert seq // CMP_STRIDE == ATT_TILE and seq % PROJ_ROW_TILE == 0
    q0 = NSA_Q_DIM
    kvd = NSA_KV_DIM
    tn = kvd
    n_q = q0 // tn
    kc, vc, ks, vs, kw, vw = range(n_q, n_q + 6)
    per_group = GQA_REP * N_GATES
    w_g = w_in[:, q0 + 6 * kvd:].reshape(D_MODEL, N_KV_HEADS, per_group)
    w_g = jnp.pad(w_g, ((0, 0), (0, 0), (0, LANES - per_group))).reshape(D_MODEL, N_KV_HEADS * LANES)
    w_g = w_g.astype(BF16)
    b_g = jnp.pad(gate_b.reshape(N_KV_HEADS, per_group), ((0, 0), (0, LANES - per_group)))
    bias_c = jnp.concatenate([jnp.zeros((2 * kvd,), F32), b_g.reshape(-1)])[None, :]
    cos, sin = _rope_tables(seq)

    q_scale = HEAD_DIM ** -0.5 * math.log2(math.e)
    oscale_qk = jnp.concatenate([jnp.full((q0,), q_scale, F32), jnp.ones((2 * kvd,), F32)])[None, :]
    w_t = w_in.T
    qk = _nsa_proj(u, w_t, w_g, jnp.zeros_like(oscale_qk), oscale_qk, cos, sin, tn=tn,
                   w_chunk=lambda j: jnp.where(j < n_q, j, ks + (kw - ks) * (j - n_q)),
                   out_dtype=BF16, rope_chunks=tuple(range(n_q + 2)), gate_chunk=None, seq=seq)
    vt = _nsa_vt(u, w_t, lambda j: vs + (vw - vs) * j)
    cvg = _nsa_proj(u, w_t, w_g, bias_c, jnp.ones_like(bias_c), cos, sin, tn=tn,
                    w_chunk=lambda j: jnp.minimum(kc + j, vc),
                    out_dtype=F32, rope_chunks=(0,), gate_chunk=2, seq=seq)
    cmp = _compress(cvg, cmp_pe, cmp_w1, cmp_b1[:, None, :], cmp_w2, cmp_b2[:, None, :], batch, seq)
    o = _nsa_attention(qk, vt, cmp, cvg, _overlap_matrix(seq), batch, seq)
    return _out_proj(o, w_out.astype(BF16), h, ga, gb)


def kernel(x, norm_g, ffn_w_up, ffn_conv_w, ffn_conv_b, ffn_w_down, pool_w_in, pool_w_grp, pool_scale,
           pool_w_out, sgu_w_in, sgu_ln_g, sgu_ln_b, sgu_w_s, sgu_b_s, sgu_w_out, nsa_w_in, nsa_gate_b,
           nsa_cmp_pe, nsa_cmp_w1, nsa_cmp_b1, nsa_cmp_w2, nsa_cmp_b2, nsa_w_out):
    batch, seq, d = x.shape
    h = x.reshape(batch * seq, d)

    def gain(i, k):
        return norm_g[i, k][None, :]

    ffn_up, ffn_down = ffn_w_up, ffn_w_down
    pool_in, pool_grp, pool_out = pool_w_in.astype(BF16), pool_w_grp.astype(BF16), pool_w_out.astype(BF16)
    gains = norm_g[:, :, None, :]
    conv_b = ffn_conv_b[:, None, :]

    u = gain(0, 0)
    for i in range(DEPTH):
        kind, j = i % 3, i // 3
        ga, gb = gain(i, 1), gain(i, 2)
        if kind == 0:
            h, u = _pool_mixer(u, h, pool_in, pool_grp, pool_scale[j][None, :], pool_out, ga, gb, j, seq)
        elif kind == 1:
            h, u = _sgu_mixer(u, h, sgu_w_in[j].astype(BF16), sgu_ln_g[j][None, :], sgu_ln_b[j][None, :],
                              sgu_w_s[j], sgu_b_s[j].T, sgu_w_out[j].astype(BF16), ga, gb)
        else:
            h, u = _nsa_mixer(u, h, nsa_w_in[j], nsa_gate_b[j], nsa_cmp_pe[j], nsa_cmp_w1[j],
                              nsa_cmp_b1[j], nsa_cmp_w2[j], nsa_cmp_b2[j], nsa_w_out[j], ga, gb,
                              batch, seq)
        next_gain = (i + 1, 0) if i + 1 < DEPTH else (i, 3)
        h, u = _conv_ffn(u, h, ffn_up, ffn_conv_w, conv_b, ffn_down, gains, i, next_gain, seq)
    return h.reshape(batch, seq, d)
```
